```python
import math
import jax, jax.numpy as jnp
from jax import lax
import numpy as np

D_MODEL = 4096
BATCH = 1
SEQ = 8192
DEPTH = 4

GRID_W = 64
CTX_LEN = 256
MIX_WIDTH = D_MODEL
CONV_WIDTH = D_MODEL // 4
SSM_WIDTH = D_MODEL // 4
ATTN_WIDTH = MIX_WIDTH - CONV_WIDTH - SSM_WIDTH
CONV_KERNEL = 31
CONV_HALF = CONV_KERNEL // 2
SSM_CH_PER_GROUP = 16
SSM_GROUPS = SSM_WIDTH // SSM_CH_PER_GROUP
SSM_STATE = 64
V_DIM = 128
N_HEADS = ATTN_WIDTH // V_DIM
QK_DIM = V_DIM // 2
ROPE_AXIS_DIM = QK_DIM // 2
ROPE_BASE = 10000.0
Q_BLOCK = 128
D_FF = 4 * D_MODEL
MOD_RANK = 512
EPS = 1e-6
IN_WIDTH = 2 * CONV_WIDTH + SSM_WIDTH + 3 * ATTN_WIDTH
IN_SPLITS = (2 * CONV_WIDTH,
             2 * CONV_WIDTH + SSM_WIDTH,
             2 * CONV_WIDTH + SSM_WIDTH + ATTN_WIDTH,
             2 * CONV_WIDTH + SSM_WIDTH + 2 * ATTN_WIDTH)

kernel_name = "hymba_style_conv_s5_diffattn_dit"


def _rms_norm(x, g):
    xf = x.astype(jnp.float32)
    y = xf * lax.rsqrt(jnp.mean(xf * xf, axis=-1, keepdims=True) + EPS)
    return (y * g.astype(jnp.float32)).astype(x.dtype)


def _layer_norm(x, g, b):
    xf = x.astype(jnp.float32)
    mu = jnp.mean(xf, axis=-1, keepdims=True)
    var = jnp.mean(jnp.square(xf - mu), axis=-1, keepdims=True)
    y = (xf - mu) * lax.rsqrt(var + EPS)
    return (y * g.astype(jnp.float32) + b.astype(jnp.float32)).astype(x.dtype)


def _modulation(s, wd, wu, b):
    m = (s @ wd) @ wu + b
    return m.reshape(m.shape[0], 1, 6, m.shape[-1] // 6)


def _modulate(h, shift, scale):
    return h * (1 + scale) + shift


def _rope_tables(n_tok):
    rows = n_tok // GRID_W
    row = jnp.repeat(jnp.arange(rows, dtype=jnp.float32), GRID_W)
    col = jnp.tile(jnp.arange(GRID_W, dtype=jnp.float32), rows)
    inv = jnp.power(ROPE_BASE, -jnp.arange(0, ROPE_AXIS_DIM, 2, dtype=jnp.float32) / ROPE_AXIS_DIM)
    ang_r = row[:, None] * inv[None]
    ang_c = col[:, None] * inv[None]
    return jnp.cos(ang_r), jnp.sin(ang_r), jnp.cos(ang_c), jnp.sin(ang_c)


def _rot(v, cos, sin):
    half = v.shape[-1] // 2
    v1, v2 = v[..., :half], v[..., half:]
    return jnp.concatenate([v1 * cos - v2 * sin, v2 * cos + v1 * sin], axis=-1)


def _rope_2d(x, tabs):
    cr, sr, cc, sc = [t[None, :, None, None, :] for t in tabs]
    xf = x.astype(jnp.float32)
    out = jnp.concatenate([_rot(xf[..., :ROPE_AXIS_DIM], cr, sr),
                           _rot(xf[..., ROPE_AXIS_DIM:], cc, sc)], axis=-1)
    return out.astype(x.dtype)


def _conformer_conv(z, dw, dw_b, ln_g, ln_b, pw, pw_b):
    a, g = jnp.split(z, 2, axis=-1)
    u = a * jax.nn.sigmoid(g)
    u = lax.conv_general_dilated(
        u, dw.astype(u.dtype)[:, None, :], window_strides=(1,),
        padding=((CONV_HALF, CONV_HALF),),
        dimension_numbers=("NWC", "WIO", "NWC"),
        feature_group_count=CONV_WIDTH) + dw_b
    u = jax.nn.silu(_layer_norm(u, ln_g, ln_b))
    return u @ pw + pw_b


def _zoh(a_re, a_im, log_step, b_re, b_im):
    a_re = a_re.astype(jnp.float32); a_im = a_im.astype(jnp.float32)
    dt = jnp.exp(log_step.astype(jnp.float32))[:, None]
    mag = jnp.exp(a_re * dt)
    lb_re = mag * jnp.cos(a_im * dt)
    lb_im = mag * jnp.sin(a_im * dt)
    nr, ni = lb_re - 1.0, lb_im
    den = a_re * a_re + a_im * a_im
    f_re = ((nr * a_re + ni * a_im) / den)[..., None]
    f_im = ((ni * a_re - nr * a_im) / den)[..., None]
    b_re = b_re.astype(jnp.float32); b_im = b_im.astype(jnp.float32)
    return lb_re, lb_im, f_re * b_re - f_im * b_im, f_re * b_im + f_im * b_re


def _drive(u, bb_re, bb_im):
    bsz, n = u.shape[0], u.shape[1]
    ug = u.astype(jnp.float32).reshape(bsz, n, SSM_GROUPS, SSM_CH_PER_GROUP)
    return (jnp.einsum("blgh,gph->blgp", ug, bb_re),
            jnp.einsum("blgh,gph->blgp", ug, bb_im))


def _complex_scan(lb_re, lb_im, bu_re, bu_im, h0, reverse):
    a_re = jnp.broadcast_to(lb_re, bu_re.shape)
    a_im = jnp.broadcast_to(lb_im, bu_re.shape)

    def combine(e1, e2):
        a1r, a1i, b1r, b1i = e1
        a2r, a2i, b2r, b2i = e2
        return (a2r * a1r - a2i * a1i, a2r * a1i + a2i * a1r,
                a2r * b1r - a2i * b1i + b2r, a2r * b1i + a2i * b1r + b2i)

    A_re, A_im, s_re, s_im = lax.associative_scan(
        combine, (a_re, a_im, bu_re, bu_im), reverse=reverse, axis=1)
    if h0 is not None:
        h_re, h_im = h0[0][:, None], h0[1][:, None]
        s_re = s_re + A_re * h_re - A_im * h_im
        s_im = s_im + A_re * h_im + A_im * h_re
    return s_re, s_im


def _readout(s_re, s_im, c_re, c_im):
    y = (jnp.einsum("blgp,ghp->blgh", s_re, c_re.astype(jnp.float32))
         - jnp.einsum("blgp,ghp->blgh", s_im, c_im.astype(jnp.float32)))
    return y.reshape(y.shape[0], y.shape[1], SSM_WIDTH)


def _s5_glu(y, w, b):
    g = jax.nn.gelu(y)
    return g * jax.nn.sigmoid(g @ w.astype(jnp.float32) + b.astype(jnp.float32))


def _s5_mixer(u, uc, a_re, a_im, log_step, b_re, b_im, c_re, c_im, d, glu_w, glu_b, with_ctx):
    df = d.astype(jnp.float32)
    y = df * u.astype(jnp.float32)
    yc = df * uc.astype(jnp.float32) if with_ctx else None
    for dr in range(2):
        rev = dr == 1
        lb_re, lb_im, bb_re, bb_im = _zoh(a_re[dr], a_im[dr], log_step[dr], b_re[dr], b_im[dr])
        cb_re, cb_im = _drive(uc, bb_re, bb_im)
        sc_re, sc_im = _complex_scan(lb_re, lb_im, cb_re, cb_im, None, rev)
        end = 0 if rev else -1
        lu_re, lu_im = _drive(u, bb_re, bb_im)
        s_re, s_im = _complex_scan(lb_re, lb_im, lu_re, lu_im,
                                   (sc_re[:, end], sc_im[:, end]), rev)
        y = y + _readout(s_re, s_im, c_re[dr], c_im[dr])
        if with_ctx:
            yc = yc + _readout(sc_re, sc_im, c_re[dr], c_im[dr])
    out = _s5_glu(y, glu_w, glu_b).astype(u.dtype)
    outc = _s5_glu(yc, glu_w, glu_b).astype(uc.dtype) if with_ctx else None
    return out, outc


def _diff_attend(qb, k, v, lam):
    s = jnp.einsum("bqhnd,bkhnd->bhnqk", qb, k,
                   preferred_element_type=jnp.float32) * (QK_DIM ** -0.5)
    p = jax.nn.softmax(s, axis=-1)
    pd = p[:, :, 0] - lam * p[:, :, 1]
    return jnp.einsum("bhqk,bkhe->bqhe", pd.astype(v.dtype), v)


def _diff_attention(zq, zk, zv, zqc, zkc, zvc, lq1, lk1, lq2, lk2, subln_g, lam_init, tabs, with_ctx):
    bsz, n = zq.shape[0], zq.shape[1]
    nc = zkc.shape[1]
    lam = (jnp.exp(jnp.sum(lq1.astype(jnp.float32) * lk1.astype(jnp.float32)))
           - jnp.exp(jnp.sum(lq2.astype(jnp.float32) * lk2.astype(jnp.float32))) + lam_init)
    q = _rope_2d(zq.reshape(bsz, n, N_HEADS, 2, QK_DIM), tabs)
    k = _rope_2d(zk.reshape(bsz, n, N_HEADS, 2, QK_DIM), tabs)
    v = zv.reshape(bsz, n, N_HEADS, V_DIM)
    kc = zkc.reshape(bsz, nc, N_HEADS, 2, QK_DIM)
    vc = zvc.reshape(bsz, nc, N_HEADS, V_DIM)
    k_all = jnp.concatenate([kc, k], axis=1)
    v_all = jnp.concatenate([vc, v], axis=1)
    nb = n // Q_BLOCK
    qblk = jnp.swapaxes(q.reshape(bsz, nb, Q_BLOCK, N_HEADS, 2, QK_DIM), 0, 1)
    o = lax.map(lambda qb: _diff_attend(qb, k_all, v_all, lam), qblk)
    o = jnp.swapaxes(o, 0, 1).reshape(bsz, n, N_HEADS, V_DIM)
    o = (_rms_norm(o, subln_g) * (1.0 - lam_init)).reshape(bsz, n, ATTN_WIDTH)
    oc = None
    if with_ctx:
        qc = zqc.reshape(bsz, nc, N_HEADS, 2, QK_DIM)
        oc = _diff_attend(qc, kc, vc, lam)
        oc = (_rms_norm(oc, subln_g) * (1.0 - lam_init)).reshape(bsz, nc, ATTN_WIDTH)
    return o, oc


def _sq_relu_mlp(h, w1, w2):
    return jnp.square(jax.nn.relu(h @ w1)) @ w2


def setup_inputs(seed: int = 0) -> dict:
    key = jax.random.key(seed)
    ks = iter(jax.random.split(key, 40))

    def nrm(shape, scale):
        return jax.random.normal(next(ks), shape, jnp.float32) * scale

    L, G, P, HG = DEPTH, SSM_GROUPS, SSM_STATE, SSM_CH_PER_GROUP
    n_idx = jnp.arange(P, dtype=jnp.float32)
    a_re = -0.5 + nrm((L, 2, G, P), 0.01)
    a_im = math.pi * n_idx[None, None, None, :] + nrm((L, 2, G, P), 0.01)
    log_step = jax.random.uniform(next(ks), (L, 2, G), jnp.float32,
                                  math.log(1e-3), math.log(1e-1))
    return {
        "x": nrm((BATCH, SEQ, D_MODEL), 1.0),
        "c": nrm((BATCH, D_MODEL), 1.0),
        "ctx": nrm((BATCH, CTX_LEN, D_MODEL), 1.0),
        "c_ctx": nrm((D_MODEL,), 1.0),
        "mod_down": nrm((L, D_MODEL, MOD_RANK), D_MODEL ** -0.5),
        "mod_up": nrm((L, MOD_RANK, 6 * D_MODEL), 0.5 * MOD_RANK ** -0.5),
        "mod_b": nrm((L, 6 * D_MODEL), 0.02),
        "norm_g": 1.0 + nrm((L, 4, D_MODEL), 0.02),
        "w_in": nrm((L, D_MODEL, IN_WIDTH), D_MODEL ** -0.5),
        "conv_dw": nrm((L, CONV_KERNEL, CONV_WIDTH), CONV_KERNEL ** -0.5),
        "conv_dw_b": nrm((L, CONV_WIDTH), 0.02),
        "conv_ln_g": 1.0 + nrm((L, CONV_WIDTH), 0.02),
        "conv_ln_b": nrm((L, CONV_WIDTH), 0.02),
        "conv_pw": nrm((L, CONV_WIDTH, CONV_WIDTH), CONV_WIDTH ** -0.5),
        "conv_pw_b": nrm((L, CONV_WIDTH), 0.02),
        "ssm_a_re": a_re,
        "ssm_a_im": a_im,
        "ssm_log_step": log_step,
        "ssm_b_re": nrm((L, 2, G, P, HG), (2 * HG) ** -0.5),
        "ssm_b_im": nrm((L, 2, G, P, HG), (2 * HG) ** -0.5),
        "ssm_c_re": nrm((L, 2, G, HG, P), (2 * P) ** -0.5),
        "ssm_c_im": nrm((L, 2, G, HG, P), (2 * P) ** -0.5),
        "ssm_d": nrm((L, SSM_WIDTH), 1.0),
        "ssm_glu_w": nrm((L, SSM_WIDTH, SSM_WIDTH), SSM_WIDTH ** -0.5),
        "ssm_glu_b": nrm((L, SSM_WIDTH), 0.02),
        "lam_q1": nrm((L, QK_DIM), 0.1),
        "lam_k1": nrm((L, QK_DIM), 0.1),
        "lam_q2": nrm((L, QK_DIM), 0.1),
        "lam_k2": nrm((L, QK_DIM), 0.1),
        "attn_subln_g": 1.0 + nrm((L, V_DIM), 0.02),
        "w_out": nrm((L, MIX_WIDTH, D_MODEL), MIX_WIDTH ** -0.5),
        "mlp_w1": nrm((L, D_MODEL, D_FF), D_MODEL ** -0.5),
        "mlp_w2": nrm((L, D_FF, D_MODEL), D_FF ** -0.5),
    }


def reference(x, c, ctx, c_ctx, mod_down, mod_up, mod_b, norm_g, w_in,
              conv_dw, conv_dw_b, conv_ln_g, conv_ln_b, conv_pw, conv_pw_b,
              ssm_a_re, ssm_a_im, ssm_log_step, ssm_b_re, ssm_b_im, ssm_c_re, ssm_c_im,
              ssm_d, ssm_glu_w, ssm_glu_b, lam_q1, lam_k1, lam_q2, lam_k2, attn_subln_g,
              w_out, mlp_w1, mlp_w2):
    tabs = _rope_tables(x.shape[1])
    xc = ctx
    s_lat = jax.nn.silu(c)
    s_ctx = jax.nn.silu(c_ctx)[None]
    for l in range(DEPTH):
        with_ctx = l < DEPTH - 1
        lam_init = 0.8 - 0.6 * math.exp(-0.3 * l)
        m = _modulation(s_lat, mod_down[l], mod_up[l], mod_b[l])
        mc = _modulation(s_ctx, mod_down[l], mod_up[l], mod_b[l])

        h = _modulate(_rms_norm(x, norm_g[l, 0]), m[:, :, 0], m[:, :, 1])
        hc = _modulate(_rms_norm(xc, norm_g[l, 0]), mc[:, :, 0], mc[:, :, 1])
        z_conv, z_ssm, z_q, z_k, z_v = jnp.split(h @ w_in[l], IN_SPLITS, axis=-1)
        zc_conv, zc_ssm, zc_q, zc_k, zc_v = jnp.split(hc @ w_in[l], IN_SPLITS, axis=-1)

        y_conv = _conformer_conv(z_conv, conv_dw[l], conv_dw_b[l], conv_ln_g[l], conv_ln_b[l],
                                 conv_pw[l], conv_pw_b[l])
        y_ssm, yc_ssm = _s5_mixer(z_ssm, zc_ssm, ssm_a_re[l], ssm_a_im[l], ssm_log_step[l],
                                  ssm_b_re[l], ssm_b_im[l], ssm_c_re[l], ssm_c_im[l],
                                  ssm_d[l], ssm_glu_w[l], ssm_glu_b[l], with_ctx)
        y_att, yc_att = _diff_attention(z_q, z_k, z_v, zc_q, zc_k, zc_v,
                                        lam_q1[l], lam_k1[l], lam_q2[l], lam_k2[l],
                                        attn_subln_g[l], lam_init, tabs, with_ctx)
        y = jnp.concatenate([y_conv, y_ssm, y_att], axis=-1) @ w_out[l]
        x = x + m[:, :, 2] * _rms_norm(y, norm_g[l, 1])
        if with_ctx:
            yc_conv = _conformer_conv(zc_conv, conv_dw[l], conv_dw_b[l], conv_ln_g[l],
                                      conv_ln_b[l], conv_pw[l], conv_pw_b[l])
            yc = jnp.concatenate([yc_conv, yc_ssm, yc_att], axis=-1) @ w_out[l]
            xc = xc + mc[:, :, 2] * _rms_norm(yc, norm_g[l, 1])

        h = _modulate(_rms_norm(x, norm_g[l, 2]), m[:, :, 3], m[:, :, 4])
        x = x + m[:, :, 5] * _rms_norm(_sq_relu_mlp(h, mlp_w1[l], mlp_w2[l]), norm_g[l, 3])
        if with_ctx:
            hc = _modulate(_rms_norm(xc, norm_g[l, 2]), mc[:, :, 3], mc[:, :, 4])
            xc = xc + mc[:, :, 5] * _rms_norm(_sq_relu_mlp(hc, mlp_w1[l], mlp_w2[l]),
                                              norm_g[l, 3])
    return x
```

```python
import functools
import math

import jax
import jax.numpy as jnp
from jax import lax
from jax.experimental import pallas as pl
from jax.experimental.pallas import tpu as pltpu

F32 = jnp.float32
BF16 = jnp.bfloat16

D_MODEL = 4096
SEQ = 8192
DEPTH = 4
GRID_W = 64
CTX_LEN = 256
ROWS = CTX_LEN + SEQ
CONV_WIDTH = 1024
SSM_WIDTH = 1024
ATTN_WIDTH = 2048
CONV_KERNEL = 31
CONV_HALF = CONV_KERNEL // 2
SSM_CH = 16
SSM_GROUPS = 64
SSM_STATE = 64
V_DIM = 128
N_HEADS = 16
QK_DIM = 64
ROPE_AXIS_DIM = 32
ROPE_BASE = 10000.0
D_FF = 4 * D_MODEL
EPS = 1e-6
IN_WIDTH = 2 * CONV_WIDTH + SSM_WIDTH + 3 * ATTN_WIDTH

LANES = 128
BF16_SUBLANES = 16
VMEM_LIMIT = 56 * 1024 * 1024

ROW_TILE = 128
MM_TM = 1056
HALO = 16
CONV_ROWS = 32
SSM_T = 32
SSM_CW = SSM_T * SSM_CH
SSM_NCH = ROWS // SSM_T
SSM_NCH_PAD = 272
SSM_GB = 8
ATT_TQ = 256
ATT_TK = 256
ATT_NKC = ROWS // ATT_TK


def _cparams(*sem):
    return pltpu.CompilerParams(dimension_semantics=sem, vmem_limit_bytes=VMEM_LIMIT)


def _mm_kernel(a_ref, b_ref, o_ref, *scratch, nk, act):
    def finish(acc):
        if act == "sqrelu":
            acc = jnp.square(jnp.maximum(acc, 0.0))
        o_ref[...] = acc.astype(o_ref.dtype)

    prod = jnp.dot(a_ref[...].astype(BF16), b_ref[...].astype(BF16), preferred_element_type=F32)
    if nk == 1:
        finish(prod)
        return
    acc_ref, = scratch
    k = pl.program_id(2)

    @pl.when(k == 0)
    def _():
        acc_ref[...] = prod

    @pl.when(k > 0)
    def _():
        acc_ref[...] += prod

    @pl.when(k == nk - 1)
    def _():
        finish(acc_ref[...])


def _matmul(a, b, *, tm, tn, tk, out_dtype, act=None, name):
    m, kdim = a.shape
    _, n = b.shape
    assert m % tm == 0 and n % tn == 0 and kdim % tk == 0
    nk = kdim // tk
    scratch = [pltpu.VMEM((tm, tn), F32)] if nk > 1 else []
    return pl.pallas_call(
        functools.partial(_mm_kernel, nk=nk, act=act),
        grid=(m // tm, n // tn, nk),
        in_specs=[pl.BlockSpec((tm, tk), lambda i, j, k: (i, k)),
                  pl.BlockSpec((tk, tn), lambda i, j, k: (k, j))],
        out_specs=pl.BlockSpec((tm, tn), lambda i, j, k: (i, j)),
        out_shape=jax.ShapeDtypeStruct((m, n), out_dtype),
        scratch_shapes=scratch,
        compiler_params=_cparams("parallel", "parallel", "arbitrary"),
        name=name,
    )(a, b)


def _mod_index(i):
    return jnp.minimum(i, 1) if CTX_LEN == ROW_TILE else (i >= CTX_LEN // ROW_TILE).astype(jnp.int32)


def _rms(x, g):
    return x * lax.rsqrt(jnp.mean(x * x, axis=-1, keepdims=True) + EPS) * g


def _normmod_kernel(x_ref, g_ref, mod_ref, h_ref, *, shift_row, scale_row):
    mod = mod_ref[0]
    y = _rms(x_ref[...], g_ref[...])
    h = y * (1.0 + mod[scale_row:scale_row + 1]) + mod[shift_row:shift_row + 1]
    h_ref[...] = h.astype(h_ref.dtype)


def _normmod(x, g, mod, *, shift_row, scale_row):
    r, d = x.shape
    return pl.pallas_call(
        functools.partial(_normmod_kernel, shift_row=shift_row, scale_row=scale_row),
        grid=(r // ROW_TILE,),
        in_specs=[pl.BlockSpec((ROW_TILE, d), lambda i: (i, 0)),
                  pl.BlockSpec((1, d), lambda i: (0, 0)),
                  pl.BlockSpec((1, 8, d), lambda i: (_mod_index(i), 0, 0))],
        out_specs=pl.BlockSpec((ROW_TILE, d), lambda i: (i, 0)),
        out_shape=jax.ShapeDtypeStruct((r, d), BF16),
        compiler_params=_cparams("parallel"),
        name="normmod",
    )(x, g.reshape(1, d), mod)


def _resid_kernel(y_ref, x_ref, ga_ref, gb_ref, moda_ref, modb_ref, xo_ref, *h_refs,
                  gate_row, shift_row, scale_row):
    gate = moda_ref[0][gate_row:gate_row + 1]
    xn = x_ref[...] + gate * _rms(y_ref[...], ga_ref[...])
    xo_ref[...] = xn
    if h_refs:
        modb = modb_ref[0]
        h = _rms(xn, gb_ref[...]) * (1.0 + modb[scale_row:scale_row + 1]) + modb[shift_row:shift_row + 1]
        h_refs[0][...] = h.astype(h_refs[0].dtype)


def _resid(y, x, ga, gb, moda, modb, *, gate_row, shift_row, scale_row, emit_h):
    r, d = x.shape
    row = pl.BlockSpec((ROW_TILE, d), lambda i: (i, 0))
    vec = pl.BlockSpec((1, d), lambda i: (0, 0))
    modspec = pl.BlockSpec((1, 8, d), lambda i: (_mod_index(i), 0, 0))
    out_shape = [jax.ShapeDtypeStruct((r, d), F32)]
    out_specs = [row]
    if emit_h:
        out_shape.append(jax.ShapeDtypeStruct((r, d), BF16))
        out_specs.append(row)
    outs = pl.pallas_call(
        functools.partial(_resid_kernel, gate_row=gate_row, shift_row=shift_row, scale_row=scale_row),
        grid=(r // ROW_TILE,),
        in_specs=[row, row, vec, vec, modspec, modspec],
        out_specs=out_specs,
        out_shape=out_shape,
        compiler_params=_cparams("parallel"),
        name="resid",
    )(y, x, ga.reshape(1, d), gb.reshape(1, d), moda, modb)
    return outs if emit_h else (outs[0], None)


def _conv_kernel(ap_ref, ac_ref, an_ref, gp_ref, gc_ref, gn_ref, dw_ref, dwb_ref, lng_ref, lnb_ref,
                 pw_ref, pwb_ref, o_ref, ub_ref, cb_ref, *, nblk, ctx_blocks):
    t = ROW_TILE
    i = pl.program_id(0)
    prev_ok = jnp.logical_and(i != 0, i != ctx_blocks)
    next_ok = jnp.logical_and(i != ctx_blocks - 1, i != nblk - 1)

    def glu(a_ref, g_ref):
        return a_ref[...] * jax.nn.sigmoid(g_ref[...])

    ub_ref[0:HALO, :] = jnp.where(prev_ok, glu(ap_ref, gp_ref), 0.0)
    ub_ref[HALO:HALO + t, :] = glu(ac_ref, gc_ref)
    ub_ref[HALO + t:2 * HALO + t, :] = jnp.where(next_ok, glu(an_ref, gn_ref), 0.0)

    base = HALO - CONV_HALF
    for c in range(CONV_WIDTH // LANES):
        cs = slice(c * LANES, (c + 1) * LANES)
        bias = jnp.broadcast_to(dwb_ref[:, cs], (CONV_ROWS, LANES))
        accs = [bias for _ in range(t // CONV_ROWS)]
        for k in range(CONV_KERNEL):
            wk = jnp.broadcast_to(dw_ref[k:k + 1, cs], (CONV_ROWS, LANES))
            for r in range(t // CONV_ROWS):
                lo = r * CONV_ROWS + base + k
                accs[r] = accs[r] + wk * ub_ref[lo:lo + CONV_ROWS, cs]
        for r in range(t // CONV_ROWS):
            cb_ref[r * CONV_ROWS:(r + 1) * CONV_ROWS, cs] = accs[r]

    cv = cb_ref[...]
    mu = jnp.mean(cv, axis=-1, keepdims=True)
    var = jnp.mean(jnp.square(cv - mu), axis=-1, keepdims=True)
    y = (cv - mu) * lax.rsqrt(var + EPS) * lng_ref[...] + lnb_ref[...]
    y = y * jax.nn.sigmoid(y)
    out = jnp.dot(y.astype(BF16), pw_ref[...], preferred_element_type=F32) + pwb_ref[...]
    o_ref[...] = out.astype(o_ref.dtype)


def _conv_mixer(z, dw, dw_b, ln_g, ln_b, pw_bf16, pw_b):
    r = z.shape[0]
    t, cw = ROW_TILE, CONV_WIDTH
    nblk = r // t
    hb = t // HALO
    nhb = r // HALO

    def cur(col):
        return pl.BlockSpec((t, cw), lambda i: (i, col))

    def prev(col):
        return pl.BlockSpec((HALO, cw), lambda i: (jnp.maximum(i * hb - 1, 0), col))

    def nxt(col):
        return pl.BlockSpec((HALO, cw), lambda i: (jnp.minimum((i + 1) * hb, nhb - 1), col))

    def whole(shape):
        return pl.BlockSpec(shape, lambda i: (0,) * len(shape))

    return pl.pallas_call(
        functools.partial(_conv_kernel, nblk=nblk, ctx_blocks=CTX_LEN // t),
        grid=(nblk,),
        in_specs=[prev(0), cur(0), nxt(0), prev(1), cur(1), nxt(1),
                  whole((CONV_KERNEL, cw)), whole((1, cw)), whole((1, cw)), whole((1, cw)),
                  whole((cw, cw)), whole((1, cw))],
        out_specs=pl.BlockSpec((t, cw), lambda i: (i, 0)),
        out_shape=jax.ShapeDtypeStruct((r, cw), BF16),
        scratch_shapes=[pltpu.VMEM((t + 2 * HALO, cw), F32), pltpu.VMEM((t, cw), F32)],
        compiler_params=_cparams("parallel"),
        name="conv_mixer",
    )(z, z, z, z, z, z, dw, dw_b.reshape(1, cw), ln_g.reshape(1, cw), ln_b.reshape(1, cw),
      pw_bf16, pw_b.reshape(1, cw))


def _ssm_operators(a_re, a_im, log_step, b_re, b_im, c_re, c_im):
    t = SSM_T
    hp = lax.Precision.HIGHEST
    a_re = a_re.astype(F32)
    a_im = a_im.astype(F32)
    dt = jnp.exp(log_step.astype(F32))[..., None]
    mag = jnp.exp(a_re * dt)
    lb_re = mag * jnp.cos(a_im * dt)
    lb_im = mag * jnp.sin(a_im * dt)
    nr, ni = lb_re - 1.0, lb_im
    den = a_re * a_re + a_im * a_im
    f_re = ((nr * a_re + ni * a_im) / den)[..., None]
    f_im = ((ni * a_re - nr * a_im) / den)[..., None]
    b_re = b_re.astype(F32)
    b_im = b_im.astype(F32)
    bb_re = f_re * b_re - f_im * b_im
    bb_im = f_re * b_im + f_im * b_re
    c_re = c_re.astype(F32)
    c_im = c_im.astype(F32)

    k = jnp.arange(t + 1, dtype=F32)[:, None, None, None]
    pmag = jnp.exp(k * (a_re * dt)[None])
    e_re = pmag * jnp.cos(k * (a_im * dt)[None])
    e_im = pmag * jnp.sin(k * (a_im * dt)[None])

    ce_re = c_re[None] * e_re[:, :, :, None, :] - c_im[None] * e_im[:, :, :, None, :]
    ce_im = c_re[None] * e_im[:, :, :, None, :] + c_im[None] * e_re[:, :, :, None, :]
    kk = (jnp.einsum("kdghp,dgpj->kdghj", ce_re, bb_re, precision=hp)
          - jnp.einsum("kdghp,dgpj->kdghj", ce_im, bb_im, precision=hp))

    jj = jnp.arange(t)[:, None]
    tt = jnp.arange(t)[None, :]
    lag = tt - jj
    sel = jnp.where((lag >= 0)[:, :, None, None, None, None], kk[jnp.clip(lag, 0, t)], 0.0)
    mt = jnp.transpose(sel, (2, 3, 0, 5, 1, 4)).reshape(2, SSM_GROUPS, SSM_CW, SSM_CW)

    er = e_re[t - 1::-1][:t]
    ei = e_im[t - 1::-1][:t]
    w_re = er[..., None] * bb_re[None] - ei[..., None] * bb_im[None]
    w_im = er[..., None] * bb_im[None] + ei[..., None] * bb_re[None]
    w_re = jnp.transpose(w_re, (1, 2, 0, 4, 3)).reshape(2, SSM_GROUPS, SSM_CW, SSM_STATE)
    w_im = jnp.transpose(w_im, (1, 2, 0, 4, 3)).reshape(2, SSM_GROUPS, SSM_CW, SSM_STATE)

    va = jnp.transpose(ce_re[1:], (1, 2, 4, 0, 3)).reshape(2, SSM_GROUPS, SSM_STATE, SSM_CW)
    vb = jnp.transpose(-ce_im[1:], (1, 2, 4, 0, 3)).reshape(2, SSM_GROUPS, SSM_STATE, SSM_CW)

    ngb = SSM_GROUPS // SSM_GB
    decay = jnp.stack([e_re[t].reshape(2, ngb, SSM_GB * SSM_STATE),
                       e_im[t].reshape(2, ngb, SSM_GB * SSM_STATE)], axis=2)
    return mt.astype(BF16), w_re.astype(BF16), w_im.astype(BF16), va.astype(BF16), vb.astype(BF16), decay


def _ssm_kernel(u_ref, mt_ref, wre_ref, wim_ref, va_ref, vb_ref, decay_ref, y_ref,
                zre_ref, zim_ref, sre_ref, sim_ref):
    p = SSM_STATE
    nch = u_ref.shape[2]
    for g in range(SSM_GB):
        u = u_ref[0, g]
        zre_ref[:, g * p:(g + 1) * p] = jnp.dot(u, wre_ref[0, g], preferred_element_type=F32)
        zim_ref[:, g * p:(g + 1) * p] = jnp.dot(u, wim_ref[0, g], preferred_element_type=F32)

    lr = decay_ref[0, 0, 0:1, :]
    li = decay_ref[0, 0, 1:2, :]

    def step(c, carry):
        sr, si = carry
        sre_ref[pl.ds(c, 1), :] = sr
        sim_ref[pl.ds(c, 1), :] = si
        zr = zre_ref[pl.ds(c, 1), :]
        zi = zim_ref[pl.ds(c, 1), :]
        return lr * sr - li * si + zr, lr * si + li * sr + zi

    zero = jnp.zeros((1, SSM_GB * p), F32)
    lax.fori_loop(0, nch, step, (zero, zero))

    for g in range(SSM_GB):
        gs = slice(g * p, (g + 1) * p)
        y = jnp.dot(u_ref[0, g], mt_ref[0, g], preferred_element_type=F32)
        y = y + jnp.dot(sre_ref[:, gs].astype(BF16), va_ref[0, g], preferred_element_type=F32)
        y = y + jnp.dot(sim_ref[:, gs].astype(BF16), vb_ref[0, g], preferred_element_type=F32)
        y_ref[0, g] = y


def _ssm_core(u_chunks, ops):
    mt, w_re, w_im, va, vb, decay = ops
    ndir, ng, nch, cw = u_chunks.shape
    gb, p = SSM_GB, SSM_STATE

    def blk(shape):
        return pl.BlockSpec((1, gb) + shape, lambda d, j: (d, j, 0, 0))

    return pl.pallas_call(
        _ssm_kernel,
        grid=(ndir, ng // gb),
        in_specs=[blk((nch, cw)), blk((cw, cw)), blk((cw, p)), blk((cw, p)), blk((p, cw)), blk((p, cw)),
                  pl.BlockSpec((1, 1, 2, gb * p), lambda d, j: (d, j, 0, 0))],
        out_specs=blk((nch, cw)),
        out_shape=jax.ShapeDtypeStruct((ndir, ng, nch, cw), F32),
        scratch_shapes=[pltpu.VMEM((nch, gb * p), F32) for _ in range(4)],
        compiler_params=_cparams("parallel", "parallel"),
        name="ssm_core",
    )(u_chunks, mt, w_re, w_im, va, vb, decay)


def _flip_segments(a):
    return jnp.concatenate([a[:CTX_LEN][::-1], a[CTX_LEN:][::-1]], axis=0)


def _to_chunks(u):
    r = u.shape[0]
    a = u.reshape(r, SSM_GROUPS, SSM_CH).transpose(1, 0, 2).reshape(SSM_GROUPS, r // SSM_T, SSM_CW)
    return jnp.pad(a, ((0, 0), (0, SSM_NCH_PAD - r // SSM_T), (0, 0)))


def _from_chunks(y):
    g, _, _ = y.shape
    a = y[:, :SSM_NCH].reshape(g, ROWS, SSM_CH).transpose(1, 0, 2)
    return a.reshape(ROWS, g * SSM_CH)


def _ssm_glu_kernel(u_ref, yf_ref, yb_ref, d_ref, w_ref, b_ref, o_ref):
    y = d_ref[...] * u_ref[...] + yf_ref[...] + yb_ref[...]
    g = jax.nn.gelu(y)
    gate = jnp.dot(g.astype(BF16), w_ref[...], preferred_element_type=F32) + b_ref[...]
    o_ref[...] = (g * jax.nn.sigmoid(gate)).astype(o_ref.dtype)


def _ssm_glu(z, yf, yb, d, w_bf16, b):
    r = z.shape[0]
    t, sw = ROW_TILE, SSM_WIDTH
    ucol = 2 * CONV_WIDTH // sw
    row = pl.BlockSpec((t, sw), lambda i: (i, 0))
    vec = pl.BlockSpec((1, sw), lambda i: (0, 0))
    return pl.pallas_call(
        _ssm_glu_kernel,
        grid=(r // t,),
        in_specs=[pl.BlockSpec((t, sw), lambda i: (i, ucol)), row, row, vec,
                  pl.BlockSpec((sw, sw), lambda i: (0, 0)), vec],
        out_specs=row,
        out_shape=jax.ShapeDtypeStruct((r, sw), BF16),
        compiler_params=_cparams("parallel"),
        name="ssm_glu",
    )(z, yf, yb, d.reshape(1, sw), w_bf16, b.reshape(1, sw))


def _s5_mixer(z, ops, d, glu_w_bf16, glu_b):
    u = z[:, 2 * CONV_WIDTH:2 * CONV_WIDTH + SSM_WIDTH].astype(BF16)
    u_chunks = jnp.stack([_to_chunks(u), _to_chunks(_flip_segments(u))])
    y = _ssm_core(u_chunks, ops)
    yf = _from_chunks(y[0])
    yb = _flip_segments(_from_chunks(y[1]))
    return _ssm_glu(z, yf, yb, d, glu_w_bf16, glu_b)


def _rope_tables():
    rows = SEQ // GRID_W
    row = jnp.repeat(jnp.arange(rows, dtype=F32), GRID_W)
    col = jnp.tile(jnp.arange(GRID_W, dtype=F32), rows)
    inv = jnp.power(ROPE_BASE, -jnp.arange(0, ROPE_AXIS_DIM, 2, dtype=F32) / ROPE_AXIS_DIM)
    ang_r = row[:, None] * inv[None]
    ang_c = col[:, None] * inv[None]
    cos_map = jnp.concatenate([jnp.cos(ang_r), jnp.cos(ang_r), jnp.cos(ang_c), jnp.cos(ang_c)], axis=-1)
    sin_map = jnp.concatenate([-jnp.sin(ang_r), jnp.sin(ang_r), -jnp.sin(ang_c), jnp.sin(ang_c)], axis=-1)
    cos_t = jnp.concatenate([cos_map, cos_map], axis=-1)
    sin_t = jnp.concatenate([sin_map, sin_map], axis=-1)
    cos_t = jnp.concatenate([jnp.ones((CTX_LEN, V_DIM), F32), cos_t], axis=0)
    sin_t = jnp.concatenate([jnp.zeros((CTX_LEN, V_DIM), F32), sin_t], axis=0)
    return cos_t, sin_t


def _rope_kernel(zq_ref, zk_ref, zv_ref, cos_ref, sin_ref, q_ref, k_ref, v_ref):
    cos = cos_ref[...]
    sin = sin_ref[...]
    lane = lax.broadcasted_iota(jnp.int32, cos.shape, 1)
    first_of_pair = (lane & (ROPE_AXIS_DIM // 2)) == 0
    map0 = lane < QK_DIM
    half = ROPE_AXIS_DIM // 2

    def rot(x):
        partner = jnp.where(first_of_pair, pltpu.roll(x, V_DIM - half, axis=1), pltpu.roll(x, half, axis=1))
        return x * cos + partner * sin

    for h in range(zq_ref.shape[1] // V_DIM):
        hs = slice(h * V_DIM, (h + 1) * V_DIM)
        q = rot(zq_ref[:, hs]) * (QK_DIM ** -0.5)
        q_ref[0, :, hs] = jnp.where(map0, q, 0.0).astype(q_ref.dtype)
        q_ref[1, :, hs] = jnp.where(map0, 0.0, q).astype(q_ref.dtype)
        k_ref[:, hs] = rot(zk_ref[:, hs]).astype(k_ref.dtype)
    v_ref[...] = zv_ref[...].astype(v_ref.dtype)


def _rope(z, cos_t, sin_t):
    r = z.shape[0]
    t, w = ROW_TILE, 1024
    qc = (2 * CONV_WIDTH + SSM_WIDTH) // w
    kc = qc + ATTN_WIDTH // w
    vc = kc + ATTN_WIDTH // w
    tab = pl.BlockSpec((t, V_DIM), lambda i, j: (i, 0))
    return pl.pallas_call(
        _rope_kernel,
        grid=(r // t, ATTN_WIDTH // w),
        in_specs=[pl.BlockSpec((t, w), lambda i, j: (i, qc + j)),
                  pl.BlockSpec((t, w), lambda i, j: (i, kc + j)),
                  pl.BlockSpec((t, w), lambda i, j: (i, vc + j)), tab, tab],
        out_specs=[pl.BlockSpec((2, t, w), lambda i, j: (0, i, j)),
                   pl.BlockSpec((t, w), lambda i, j: (i, j)),
                   pl.BlockSpec((t, w), lambda i, j: (i, j))],
        out_shape=[jax.ShapeDtypeStruct((2, r, ATTN_WIDTH), BF16),
                   jax.ShapeDtypeStruct((r, ATTN_WIDTH), BF16),
                   jax.ShapeDtypeStruct((r, ATTN_WIDTH), BF16)],
        compiler_params=_cparams("parallel", "parallel"),
        name="rope",
    )(z, z, z, cos_t, sin_t)


def _attn_kernel(lam_ref, q_ref, kt_ref, v_ref, g_ref, o_ref, m_ref, l_ref, acc_ref, *, out_scale):
    tq = ATT_TQ
    iq = pl.program_id(1)
    q = q_ref[...].reshape(2 * tq, V_DIM)
    m_ref[...] = jnp.full(m_ref.shape, -jnp.inf, F32)
    l_ref[...] = jnp.zeros(l_ref.shape, F32)
    acc_ref[...] = jnp.zeros(acc_ref.shape, F32)

    def step(j, carry):
        s = jnp.dot(q, kt_ref[0, j], preferred_element_type=F32)
        m_old = m_ref[...]
        m_new = jnp.maximum(m_old, jnp.max(s, axis=-1, keepdims=True))
        alpha = jnp.exp(m_old - m_new)
        p = jnp.exp(s - m_new)
        l_ref[...] = alpha * l_ref[...] + jnp.sum(p, axis=-1, keepdims=True)
        acc_ref[...] = alpha * acc_ref[...] + jnp.dot(p.astype(BF16), v_ref[pl.ds(j * ATT_TK, ATT_TK), :],
                                                      preferred_element_type=F32)
        m_ref[...] = m_new
        return carry

    nkc = jnp.where(iq < CTX_LEN // tq, CTX_LEN // ATT_TK, ATT_NKC)
    lax.fori_loop(0, nkc, step, 0)

    o = acc_ref[...] / l_ref[...]
    od = o[:tq] - lam_ref[0] * o[tq:]
    od = od * lax.rsqrt(jnp.mean(od * od, axis=-1, keepdims=True) + EPS) * g_ref[...]
    o_ref[...] = (od * out_scale).astype(o_ref.dtype)


def _attention(q2, kt, v, lam, subln_g, lam_init):
    r = v.shape[0]
    tq = ATT_TQ
    return pl.pallas_call(
        functools.partial(_attn_kernel, out_scale=1.0 - lam_init),
        grid=(N_HEADS, r // tq),
        in_specs=[pl.BlockSpec(memory_space=pltpu.SMEM),
                  pl.BlockSpec((2, tq, V_DIM), lambda h, i: (0, i, h)),
                  pl.BlockSpec((1, ATT_NKC, V_DIM, ATT_TK), lambda h, i: (h, 0, 0, 0)),
                  pl.BlockSpec((r, V_DIM), lambda h, i: (0, h)),
                  pl.BlockSpec((1, V_DIM), lambda h, i: (0, 0))],
        out_specs=pl.BlockSpec((tq, V_DIM), lambda h, i: (i, h)),
        out_shape=jax.ShapeDtypeStruct((r, ATTN_WIDTH), BF16),
        scratch_shapes=[pltpu.VMEM((2 * tq, 1), F32), pltpu.VMEM((2 * tq, 1), F32),
                        pltpu.VMEM((2 * tq, V_DIM), F32)],
        compiler_params=_cparams("parallel", "arbitrary"),
        name="diff_attention",
    )(lam.reshape(1), q2, kt, v, subln_g.reshape(1, V_DIM))


def _diff_attention(z, cos_t, sin_t, lq1, lk1, lq2, lk2, subln_g, lam_init):
    lam = (jnp.exp(jnp.sum(lq1.astype(F32) * lk1.astype(F32)))
           - jnp.exp(jnp.sum(lq2.astype(F32) * lk2.astype(F32))) + lam_init)
    q2, k_r, v = _rope(z, cos_t, sin_t)
    kt = k_r.reshape(ATT_NKC, ATT_TK, N_HEADS, V_DIM).transpose(2, 0, 3, 1)
    return _attention(q2, kt, v, lam.astype(F32), subln_g.astype(F32), lam_init)


def _modulation_tables(s_pair, mod_down, mod_up, mod_b):
    d = D_MODEL
    low = _matmul(s_pair, mod_down, tm=BF16_SUBLANES, tn=mod_down.shape[1], tk=d, out_dtype=F32, name="mod_down")
    m = _matmul(low, mod_up, tm=BF16_SUBLANES, tn=2048, tk=mod_up.shape[0], out_dtype=F32, name="mod_up")
    m = (m[:2] + mod_b[None]).reshape(2, 6, d)
    return jnp.pad(m, ((0, 0), (0, 2), (0, 0)))


def kernel(x, c, ctx, c_ctx, mod_down, mod_up, mod_b, norm_g, w_in, conv_dw, conv_dw_b, conv_ln_g, conv_ln_b, conv_pw, conv_pw_b, ssm_a_re, ssm_a_im, ssm_log_step, ssm_b_re, ssm_b_im, ssm_c_re, ssm_c_im, ssm_d, ssm_glu_w, ssm_glu_b, lam_q1, lam_k1, lam_q2, lam_k2, attn_subln_g, w_out, mlp_w1, mlp_w2):
    d = D_MODEL
    xs = jnp.concatenate([ctx[0], x[0]], axis=0).astype(F32)
    s_pair = jnp.concatenate([jax.nn.silu(c_ctx)[None], jax.nn.silu(c)], axis=0).astype(F32)
    s_pair = jnp.pad(s_pair, ((0, BF16_SUBLANES - 2), (0, 0)))
    cos_t, sin_t = _rope_tables()

    mods = [_modulation_tables(s_pair, mod_down[l], mod_up[l], mod_b[l]) for l in range(DEPTH)]
    h = _normmod(xs, norm_g[0, 0], mods[0], shift_row=0, scale_row=1)

    for l in range(DEPTH):
        lam_init = 0.8 - 0.6 * math.exp(-0.3 * l)
        mod = mods[l]

        z = _matmul(h, w_in[l].astype(BF16), tm=MM_TM, tn=512, tk=d, out_dtype=F32, name="w_in")
        y_conv = _conv_mixer(z, conv_dw[l], conv_dw_b[l], conv_ln_g[l], conv_ln_b[l],
                             conv_pw[l].astype(BF16), conv_pw_b[l])
        ops = _ssm_operators(ssm_a_re[l], ssm_a_im[l], ssm_log_step[l], ssm_b_re[l], ssm_b_im[l],
                             ssm_c_re[l], ssm_c_im[l])
        y_ssm = _s5_mixer(z, ops, ssm_d[l], ssm_glu_w[l].astype(BF16), ssm_glu_b[l])
        y_att = _diff_attention(z, cos_t, sin_t, lam_q1[l], lam_k1[l], lam_q2[l], lam_k2[l],
                                attn_subln_g[l], lam_init)
        ycat = jnp.concatenate([y_conv, y_ssm, y_att], axis=-1)
        y = _matmul(ycat, w_out[l].astype(BF16), tm=MM_TM, tn=512, tk=d, out_dtype=F32, name="w_out")
        xs, h = _resid(y, xs, norm_g[l, 1], norm_g[l, 2], mod, mod,
                       gate_row=2, shift_row=3, scale_row=4, emit_h=True)

        a = _matmul(h, mlp_w1[l].astype(BF16), tm=MM_TM, tn=512, tk=d, out_dtype=BF16, act="sqrelu",
                    name="mlp_w1")
        y = _matmul(a, mlp_w2[l].astype(BF16), tm=MM_TM, tn=1024, tk=2048, out_dtype=F32, name="mlp_w2")
        last = l == DEPTH - 1
        nxt = min(l + 1, DEPTH - 1)
        xs, h = _resid(y, xs, norm_g[l, 3], norm_g[nxt, 0], mod, mods[nxt],
                       gate_row=5, shift_row=0, scale_row=1, emit_h=not last)

    return xs[CTX_LEN:][None].astype(x.dtype)
```

```python
import functools
import math

import jax
import jax.numpy as jnp
from jax import lax
from jax.experimental import pallas as pl
from jax.experimental.pallas import tpu as pltpu

F32 = jnp.float32
BF16 = jnp.bfloat16

D_MODEL = 4096
SEQ = 8192
DEPTH = 4
GRID_W = 64
CTX_LEN = 256
ROWS = CTX_LEN + SEQ
CONV_WIDTH = 1024
SSM_WIDTH = 1024
ATTN_WIDTH = 2048
CONV_KERNEL = 31
CONV_HALF = CONV_KERNEL // 2
SSM_CH = 16
SSM_GROUPS = 64
SSM_STATE = 64
V_DIM = 128
N_HEADS = 16
QK_DIM = 64
ROPE_AXIS_DIM = 32
ROPE_BASE = 10000.0
D_FF = 4 * D_MODEL
EPS = 1e-6
IN_WIDTH = 2 * CONV_WIDTH + SSM_WIDTH + 3 * ATTN_WIDTH

LANES = 128
BF16_SUBLANES = 16
VMEM_LIMIT = 56 * 1024 * 1024

ROW_TILE = 128
MM_TM = 1056
HALO = 16
CONV_ROWS = 32
SSM_T = 32
SSM_CW = SSM_T * SSM_CH
SSM_NCH = ROWS // SSM_T
SSM_NCH_PAD = 272
SSM_GB = 8
ATT_TQ = 256
ATT_TK = 512
ATT_UNROLL = 8


def _cparams(*sem):
    return pltpu.CompilerParams(dimension_semantics=sem, vmem_limit_bytes=VMEM_LIMIT)


def _mm_kernel(a_ref, b_ref, o_ref, *scratch, nk, act):
    def finish(acc):
        if act == "sqrelu":
            acc = jnp.square(jnp.maximum(acc, 0.0))
        o_ref[...] = acc.astype(o_ref.dtype)

    prod = jnp.dot(a_ref[...].astype(BF16), b_ref[...].astype(BF16), preferred_element_type=F32)
    if nk == 1:
        finish(prod)
        return
    acc_ref, = scratch
    k = pl.program_id(2)

    @pl.when(k == 0)
    def _():
        acc_ref[...] = prod

    @pl.when(k > 0)
    def _():
        acc_ref[...] += prod

    @pl.when(k == nk - 1)
    def _():
        finish(acc_ref[...])


def _matmul(a, b, *, tm, tn, tk, out_dtype, act=None, name):
    m, kdim = a.shape
    _, n = b.shape
    assert m % tm == 0 and n % tn == 0 and kdim % tk == 0
    nk = kdim // tk
    scratch = [pltpu.VMEM((tm, tn), F32)] if nk > 1 else []
    return pl.pallas_call(
        functools.partial(_mm_kernel, nk=nk, act=act),
        grid=(m // tm, n // tn, nk),
        in_specs=[pl.BlockSpec((tm, tk), lambda i, j, k: (i, k)),
                  pl.BlockSpec((tk, tn), lambda i, j, k: (k, j))],
        out_specs=pl.BlockSpec((tm, tn), lambda i, j, k: (i, j)),
        out_shape=jax.ShapeDtypeStruct((m, n), out_dtype),
        scratch_shapes=scratch,
        compiler_params=_cparams("parallel", "parallel", "arbitrary"),
        name=name,
    )(a, b)


def _mod_index(i):
    return jnp.minimum(i, 1) if CTX_LEN == ROW_TILE else (i >= CTX_LEN // ROW_TILE).astype(jnp.int32)


def _rms(x, g):
    return x * lax.rsqrt(jnp.mean(x * x, axis=-1, keepdims=True) + EPS) * g


def _normmod_kernel(x_ref, g_ref, mod_ref, h_ref, *, shift_row, scale_row):
    mod = mod_ref[0]
    y = _rms(x_ref[...], g_ref[...])
    h = y * (1.0 + mod[scale_row:scale_row + 1]) + mod[shift_row:shift_row + 1]
    h_ref[...] = h.astype(h_ref.dtype)


def _normmod(x, g, mod, *, shift_row, scale_row):
    r, d = x.shape
    return pl.pallas_call(
        functools.partial(_normmod_kernel, shift_row=shift_row, scale_row=scale_row),
        grid=(r // ROW_TILE,),
        in_specs=[pl.BlockSpec((ROW_TILE, d), lambda i: (i, 0)),
                  pl.BlockSpec((1, d), lambda i: (0, 0)),
                  pl.BlockSpec((1, 8, d), lambda i: (_mod_index(i), 0, 0))],
        out_specs=pl.BlockSpec((ROW_TILE, d), lambda i: (i, 0)),
        out_shape=jax.ShapeDtypeStruct((r, d), BF16),
        compiler_params=_cparams("parallel"),
        name="normmod",
    )(x, g.reshape(1, d), mod)


def _resid_kernel(y_ref, x_ref, ga_ref, gb_ref, moda_ref, modb_ref, xo_ref, *h_refs,
                  gate_row, shift_row, scale_row):
    gate = moda_ref[0][gate_row:gate_row + 1]
    xn = x_ref[...] + gate * _rms(y_ref[...], ga_ref[...])
    xo_ref[...] = xn
    if h_refs:
        modb = modb_ref[0]
        h = _rms(xn, gb_ref[...]) * (1.0 + modb[scale_row:scale_row + 1]) + modb[shift_row:shift_row + 1]
        h_refs[0][...] = h.astype(h_refs[0].dtype)


def _resid(y, x, ga, gb, moda, modb, *, gate_row, shift_row, scale_row, emit_h):
    r, d = x.shape
    row = pl.BlockSpec((ROW_TILE, d), lambda i: (i, 0))
    vec = pl.BlockSpec((1, d), lambda i: (0, 0))
    modspec = pl.BlockSpec((1, 8, d), lambda i: (_mod_index(i), 0, 0))
    out_shape = [jax.ShapeDtypeStruct((r, d), F32)]
    out_specs = [row]
    if emit_h:
        out_shape.append(jax.ShapeDtypeStruct((r, d), BF16))
        out_specs.append(row)
    outs = pl.pallas_call(
        functools.partial(_resid_kernel, gate_row=gate_row, shift_row=shift_row, scale_row=scale_row),
        grid=(r // ROW_TILE,),
        in_specs=[row, row, vec, vec, modspec, modspec],
        out_specs=out_specs,
        out_shape=out_shape,
        compiler_params=_cparams("parallel"),
        name="resid",
    )(y, x, ga.reshape(1, d), gb.reshape(1, d), moda, modb)
    return outs if emit_h else (outs[0], None)


def _conv_kernel(ap_ref, ac_ref, an_ref, gp_ref, gc_ref, gn_ref, dw_ref, dwb_ref, lng_ref, lnb_ref,
                 pw_ref, pwb_ref, o_ref, ub_ref, cb_ref, *, nblk, ctx_blocks):
    t = ROW_TILE
    i = pl.program_id(0)
    prev_ok = jnp.logical_and(i != 0, i != ctx_blocks)
    next_ok = jnp.logical_and(i != ctx_blocks - 1, i != nblk - 1)

    def glu(a_ref, g_ref):
        return a_ref[...] * jax.nn.sigmoid(g_ref[...])

    ub_ref[0:HALO, :] = jnp.where(prev_ok, glu(ap_ref, gp_ref), 0.0)
    ub_ref[HALO:HALO + t, :] = glu(ac_ref, gc_ref)
    ub_ref[HALO + t:2 * HALO + t, :] = jnp.where(next_ok, glu(an_ref, gn_ref), 0.0)

    base = HALO - CONV_HALF
    for c in range(CONV_WIDTH // LANES):
        cs = slice(c * LANES, (c + 1) * LANES)
        bias = jnp.broadcast_to(dwb_ref[:, cs], (CONV_ROWS, LANES))
        accs = [bias for _ in range(t // CONV_ROWS)]
        for k in range(CONV_KERNEL):
            wk = jnp.broadcast_to(dw_ref[k:k + 1, cs], (CONV_ROWS, LANES))
            for r in range(t // CONV_ROWS):
                lo = r * CONV_ROWS + base + k
                accs[r] = accs[r] + wk * ub_ref[lo:lo + CONV_ROWS, cs]
        for r in range(t // CONV_ROWS):
            cb_ref[r * CONV_ROWS:(r + 1) * CONV_ROWS, cs] = accs[r]

    cv = cb_ref[...]
    mu = jnp.mean(cv, axis=-1, keepdims=True)
    var = jnp.mean(jnp.square(cv - mu), axis=-1, keepdims=True)
    y = (cv - mu) * lax.rsqrt(var + EPS) * lng_ref[...] + lnb_ref[...]
    y = y * jax.nn.sigmoid(y)
    out = jnp.dot(y.astype(BF16), pw_ref[...], preferred_element_type=F32) + pwb_ref[...]
    o_ref[...] = out.astype(o_ref.dtype)


def _conv_mixer(z, dw, dw_b, ln_g, ln_b, pw_bf16, pw_b):
    r = z.shape[0]
    t, cw = ROW_TILE, CONV_WIDTH
    nblk = r // t
    hb = t // HALO
    nhb = r // HALO

    def cur(col):
        return pl.BlockSpec((t, cw), lambda i: (i, col))

    def prev(col):
        return pl.BlockSpec((HALO, cw), lambda i: (jnp.maximum(i * hb - 1, 0), col))

    def nxt(col):
        return pl.BlockSpec((HALO, cw), lambda i: (jnp.minimum((i + 1) * hb, nhb - 1), col))

    def whole(shape):
        return pl.BlockSpec(shape, lambda i: (0,) * len(shape))

    return pl.pallas_call(
        functools.partial(_conv_kernel, nblk=nblk, ctx_blocks=CTX_LEN // t),
        grid=(nblk,),
        in_specs=[prev(0), cur(0), nxt(0), prev(1), cur(1), nxt(1),
                  whole((CONV_KERNEL, cw)), whole((1, cw)), whole((1, cw)), whole((1, cw)),
                  whole((cw, cw)), whole((1, cw))],
        out_specs=pl.BlockSpec((t, cw), lambda i: (i, 0)),
        out_shape=jax.ShapeDtypeStruct((r, cw), BF16),
        scratch_shapes=[pltpu.VMEM((t + 2 * HALO, cw), F32), pltpu.VMEM((t, cw), F32)],
        compiler_params=_cparams("parallel"),
        name="conv_mixer",
    )(z, z, z, z, z, z, dw, dw_b.reshape(1, cw), ln_g.reshape(1, cw), ln_b.reshape(1, cw),
      pw_bf16, pw_b.reshape(1, cw))


def _ssm_operators(a_re, a_im, log_step, b_re, b_im, c_re, c_im):
    t = SSM_T
    hp = lax.Precision.HIGHEST
    a_re = a_re.astype(F32)
    a_im = a_im.astype(F32)
    dt = jnp.exp(log_step.astype(F32))[..., None]
    mag = jnp.exp(a_re * dt)
    lb_re = mag * jnp.cos(a_im * dt)
    lb_im = mag * jnp.sin(a_im * dt)
    nr, ni = lb_re - 1.0, lb_im
    den = a_re * a_re + a_im * a_im
    f_re = ((nr * a_re + ni * a_im) / den)[..., None]
    f_im = ((ni * a_re - nr * a_im) / den)[..., None]
    b_re = b_re.astype(F32)
    b_im = b_im.astype(F32)
    bb_re = f_re * b_re - f_im * b_im
    bb_im = f_re * b_im + f_im * b_re
    c_re = c_re.astype(F32)
    c_im = c_im.astype(F32)

    k = jnp.arange(t + 1, dtype=F32)[:, None, None, None]
    pmag = jnp.exp(k * (a_re * dt)[None])
    e_re = pmag * jnp.cos(k * (a_im * dt)[None])
    e_im = pmag * jnp.sin(k * (a_im * dt)[None])

    ce_re = c_re[None] * e_re[:, :, :, None, :] - c_im[None] * e_im[:, :, :, None, :]
    ce_im = c_re[None] * e_im[:, :, :, None, :] + c_im[None] * e_re[:, :, :, None, :]
    kk = (jnp.einsum("kdghp,dgpj->kdghj", ce_re, bb_re, precision=hp)
          - jnp.einsum("kdghp,dgpj->kdghj", ce_im, bb_im, precision=hp))

    jj = jnp.arange(t)[:, None]
    tt = jnp.arange(t)[None, :]
    lag = tt - jj
    sel = jnp.where((lag >= 0)[:, :, None, None, None, None], kk[jnp.clip(lag, 0, t)], 0.0)
    mt = jnp.transpose(sel, (2, 3, 0, 5, 1, 4)).reshape(2, SSM_GROUPS, SSM_CW, SSM_CW)

    er = e_re[t - 1::-1][:t]
    ei = e_im[t - 1::-1][:t]
    w_re = er[..., None] * bb_re[None] - ei[..., None] * bb_im[None]
    w_im = er[..., None] * bb_im[None] + ei[..., None] * bb_re[None]
    w_re = jnp.transpose(w_re, (1, 2, 0, 4, 3)).reshape(2, SSM_GROUPS, SSM_CW, SSM_STATE)
    w_im = jnp.transpose(w_im, (1, 2, 0, 4, 3)).reshape(2, SSM_GROUPS, SSM_CW, SSM_STATE)

    va = jnp.transpose(ce_re[1:], (1, 2, 4, 0, 3)).reshape(2, SSM_GROUPS, SSM_STATE, SSM_CW)
    vb = jnp.transpose(-ce_im[1:], (1, 2, 4, 0, 3)).reshape(2, SSM_GROUPS, SSM_STATE, SSM_CW)

    ngb = SSM_GROUPS // SSM_GB
    decay = jnp.stack([e_re[t].reshape(2, ngb, SSM_GB * SSM_STATE),
                       e_im[t].reshape(2, ngb, SSM_GB * SSM_STATE)], axis=2)
    return mt.astype(BF16), w_re.astype(BF16), w_im.astype(BF16), va.astype(BF16), vb.astype(BF16), decay


def _ssm_kernel(u_ref, mt_ref, wre_ref, wim_ref, va_ref, vb_ref, decay_ref, y_ref,
                zre_ref, zim_ref, sre_ref, sim_ref):
    p = SSM_STATE
    nch = u_ref.shape[2]
    for g in range(SSM_GB):
        u = u_ref[0, g]
        zre_ref[:, g * p:(g + 1) * p] = jnp.dot(u, wre_ref[0, g], preferred_element_type=F32)
        zim_ref[:, g * p:(g + 1) * p] = jnp.dot(u, wim_ref[0, g], preferred_element_type=F32)

    lr = decay_ref[0, 0, 0:1, :]
    li = decay_ref[0, 0, 1:2, :]

    def step(c, carry):
        sr, si = carry
        sre_ref[pl.ds(c, 1), :] = sr
        sim_ref[pl.ds(c, 1), :] = si
        zr = zre_ref[pl.ds(c, 1), :]
        zi = zim_ref[pl.ds(c, 1), :]
        return lr * sr - li * si + zr, lr * si + li * sr + zi

    zero = jnp.zeros((1, SSM_GB * p), F32)
    lax.fori_loop(0, nch, step, (zero, zero))

    for g in range(SSM_GB):
        gs = slice(g * p, (g + 1) * p)
        y = jnp.dot(u_ref[0, g], mt_ref[0, g], preferred_element_type=F32)
        y = y + jnp.dot(sre_ref[:, gs].astype(BF16), va_ref[0, g], preferred_element_type=F32)
        y = y + jnp.dot(sim_ref[:, gs].astype(BF16), vb_ref[0, g], preferred_element_type=F32)
        y_ref[0, g] = y


def _ssm_core(u_chunks, ops):
    mt, w_re, w_im, va, vb, decay = ops
    ndir, ng, nch, cw = u_chunks.shape
    gb, p = SSM_GB, SSM_STATE

    def blk(shape):
        return pl.BlockSpec((1, gb) + shape, lambda d, j: (d, j, 0, 0))

    return pl.pallas_call(
        _ssm_kernel,
        grid=(ndir, ng // gb),
        in_specs=[blk((nch, cw)), blk((cw, cw)), blk((cw, p)), blk((cw, p)), blk((p, cw)), blk((p, cw)),
                  pl.BlockSpec((1, 1, 2, gb * p), lambda d, j: (d, j, 0, 0))],
        out_specs=blk((nch, cw)),
        out_shape=jax.ShapeDtypeStruct((ndir, ng, nch, cw), F32),
        scratch_shapes=[pltpu.VMEM((nch, gb * p), F32) for _ in range(4)],
        compiler_params=_cparams("parallel", "parallel"),
        name="ssm_core",
    )(u_chunks, mt, w_re, w_im, va, vb, decay)


def _flip_segments(a):
    return jnp.concatenate([a[:CTX_LEN][::-1], a[CTX_LEN:][::-1]], axis=0)


def _to_chunks(u):
    r = u.shape[0]
    a = u.reshape(r, SSM_GROUPS, SSM_CH).transpose(1, 0, 2).reshape(SSM_GROUPS, r // SSM_T, SSM_CW)
    return jnp.pad(a, ((0, 0), (0, SSM_NCH_PAD - r // SSM_T), (0, 0)))


def _from_chunks(y):
    g, _, _ = y.shape
    a = y[:, :SSM_NCH].reshape(g, ROWS, SSM_CH).transpose(1, 0, 2)
    return a.reshape(ROWS, g * SSM_CH)


def _ssm_glu_kernel(u_ref, yf_ref, yb_ref, d_ref, w_ref, b_ref, o_ref):
    y = d_ref[...] * u_ref[...] + yf_ref[...] + yb_ref[...]
    g = jax.nn.gelu(y)
    gate = jnp.dot(g.astype(BF16), w_ref[...], preferred_element_type=F32) + b_ref[...]
    o_ref[...] = (g * jax.nn.sigmoid(gate)).astype(o_ref.dtype)


def _ssm_glu(z, yf, yb, d, w_bf16, b):
    r = z.shape[0]
    t, sw = ROW_TILE, SSM_WIDTH
    ucol = 2 * CONV_WIDTH // sw
    row = pl.BlockSpec((t, sw), lambda i: (i, 0))
    vec = pl.BlockSpec((1, sw), lambda i: (0, 0))
    return pl.pallas_call(
        _ssm_glu_kernel,
        grid=(r // t,),
        in_specs=[pl.BlockSpec((t, sw), lambda i: (i, ucol)), row, row, vec,
                  pl.BlockSpec((sw, sw), lambda i: (0, 0)), vec],
        out_specs=row,
        out_shape=jax.ShapeDtypeStruct((r, sw), BF16),
        compiler_params=_cparams("parallel"),
        name="ssm_glu",
    )(z, yf, yb, d.reshape(1, sw), w_bf16, b.reshape(1, sw))


def _s5_mixer(z, ops, d, glu_w_bf16, glu_b):
    u = z[:, 2 * CONV_WIDTH:2 * CONV_WIDTH + SSM_WIDTH].astype(BF16)
    u_chunks = jnp.stack([_to_chunks(u), _to_chunks(_flip_segments(u))])
    y = _ssm_core(u_chunks, ops)
    yf = _from_chunks(y[0])
    yb = _flip_segments(_from_chunks(y[1]))
    return _ssm_glu(z, yf, yb, d, glu_w_bf16, glu_b)


def _rope_tables():
    rows = SEQ // GRID_W
    row = jnp.repeat(jnp.arange(rows, dtype=F32), GRID_W)
    col = jnp.tile(jnp.arange(GRID_W, dtype=F32), rows)
    inv = jnp.power(ROPE_BASE, -jnp.arange(0, ROPE_AXIS_DIM, 2, dtype=F32) / ROPE_AXIS_DIM)
    ang_r = row[:, None] * inv[None]
    ang_c = col[:, None] * inv[None]
    cos_map = jnp.concatenate([jnp.cos(ang_r), jnp.cos(ang_r), jnp.cos(ang_c), jnp.cos(ang_c)], axis=-1)
    sin_map = jnp.concatenate([-jnp.sin(ang_r), jnp.sin(ang_r), -jnp.sin(ang_c), jnp.sin(ang_c)], axis=-1)
    cos_t = jnp.concatenate([cos_map, cos_map], axis=-1)
    sin_t = jnp.concatenate([sin_map, sin_map], axis=-1)
    cos_t = jnp.concatenate([jnp.ones((CTX_LEN, V_DIM), F32), cos_t], axis=0)
    sin_t = jnp.concatenate([jnp.zeros((CTX_LEN, V_DIM), F32), sin_t], axis=0)
    return cos_t, sin_t


def _rope_kernel(zq_ref, zk_ref, zv_ref, cos_ref, sin_ref, q_ref, k_ref, v_ref):
    cos = cos_ref[...]
    sin = sin_ref[...]
    lane = lax.broadcasted_iota(jnp.int32, cos.shape, 1)
    first_of_pair = (lane & (ROPE_AXIS_DIM // 2)) == 0
    map0 = lane < QK_DIM
    half = ROPE_AXIS_DIM // 2

    def rot(x):
        partner = jnp.where(first_of_pair, pltpu.roll(x, V_DIM - half, axis=1), pltpu.roll(x, half, axis=1))
        return x * cos + partner * sin

    for h in range(zq_ref.shape[1] // V_DIM):
        hs = slice(h * V_DIM, (h + 1) * V_DIM)
        q = rot(zq_ref[:, hs]) * (QK_DIM ** -0.5)
        q_ref[0, :, hs] = jnp.where(map0, q, 0.0).astype(q_ref.dtype)
        q_ref[1, :, hs] = jnp.where(map0, 0.0, q).astype(q_ref.dtype)
        k_ref[:, hs] = rot(zk_ref[:, hs]).astype(k_ref.dtype)
    ones = jnp.ones((zv_ref.shape[0], V_DIM), v_ref.dtype)
    for h in range(zv_ref.shape[1] // V_DIM):
        v_ref[:, 2 * h * V_DIM:(2 * h + 1) * V_DIM] = zv_ref[:, h * V_DIM:(h + 1) * V_DIM].astype(v_ref.dtype)
        v_ref[:, (2 * h + 1) * V_DIM:(2 * h + 2) * V_DIM] = ones


def _rope(z, cos_t, sin_t):
    r = z.shape[0]
    t, w = ROW_TILE, 1024
    qc = (2 * CONV_WIDTH + SSM_WIDTH) // w
    kc = qc + ATTN_WIDTH // w
    vc = kc + ATTN_WIDTH // w
    tab = pl.BlockSpec((t, V_DIM), lambda i, j: (i, 0))
    return pl.pallas_call(
        _rope_kernel,
        grid=(r // t, ATTN_WIDTH // w),
        in_specs=[pl.BlockSpec((t, w), lambda i, j: (i, qc + j)),
                  pl.BlockSpec((t, w), lambda i, j: (i, kc + j)),
                  pl.BlockSpec((t, w), lambda i, j: (i, vc + j)), tab, tab],
        out_specs=[pl.BlockSpec((2, t, w), lambda i, j: (0, i, j)),
                   pl.BlockSpec((t, w), lambda i, j: (i, j)),
                   pl.BlockSpec((t, 2 * w), lambda i, j: (i, j))],
        out_shape=[jax.ShapeDtypeStruct((2, r, ATTN_WIDTH), BF16),
                   jax.ShapeDtypeStruct((r, ATTN_WIDTH), BF16),
                   jax.ShapeDtypeStruct((r, 2 * ATTN_WIDTH), BF16)],
        compiler_params=_cparams("parallel", "parallel"),
        name="rope",
    )(z, z, z, cos_t, sin_t)


def _attn_kernel(lam_ref, q_ref, ktc_ref, ktl_ref, v_ref, g_ref, o_ref, m_ref, acc_ref, s_ref, *, out_scale):
    tq = ATT_TQ
    iq = pl.program_id(1)
    q = q_ref[...].reshape(2 * tq, V_DIM)

    def scores(kt):
        return jnp.dot(q, kt, preferred_element_type=F32)

    def lane_tiles(s):
        return [s[:, c * LANES:(c + 1) * LANES] for c in range(s.shape[1] // LANES)]

    def consume_first(s, v):
        tiles = lane_tiles(s)
        m_new = jnp.broadcast_to(jnp.max(functools.reduce(jnp.maximum, tiles), axis=-1, keepdims=True),
                                 m_ref.shape)
        p = jnp.concatenate([jnp.exp(t - m_new) for t in tiles], axis=-1).astype(BF16)
        acc_ref[...] = jnp.dot(p, v, preferred_element_type=F32)
        m_ref[...] = m_new

    def consume(s, v):
        tiles = lane_tiles(s)
        m_old = m_ref[...]
        m_new = jnp.maximum(m_old, jnp.max(functools.reduce(jnp.maximum, tiles), axis=-1, keepdims=True))
        alpha = jnp.exp(m_old - m_new)
        p = jnp.concatenate([jnp.exp(t - m_new) for t in tiles], axis=-1).astype(BF16)
        pv = jnp.dot(p, v, preferred_element_type=F32)
        acc_ref[...] = jnp.concatenate([alpha, alpha], axis=-1) * acc_ref[...] + pv
        m_ref[...] = m_new

    def vrows(j):
        return v_ref[pl.ds(pl.multiple_of(CTX_LEN + j * ATT_TK, ATT_TK // 2), ATT_TK), :]

    nlc = SEQ // ATT_TK
    is_ctx = iq < CTX_LEN // tq

    @pl.when(is_ctx)
    def _():
        consume_first(scores(ktc_ref[0]), v_ref[0:CTX_LEN, :])

    @pl.when(jnp.logical_not(is_ctx))
    def _():
        s_ref[0] = scores(ktl_ref[0, 0])
        consume_first(scores(ktc_ref[0]), v_ref[0:CTX_LEN, :])

        def trip(i, carry):
            j = ATT_UNROLL * i
            for u in range(ATT_UNROLL):
                s_ref[(u + 1) % 2] = scores(ktl_ref[0, jnp.minimum(j + u + 1, nlc - 1)])
                consume(s_ref[u % 2], vrows(j + u))
            return carry

        lax.fori_loop(0, nlc // ATT_UNROLL, trip, 0)

    o = acc_ref[:, :V_DIM] / acc_ref[:, V_DIM:]
    od = o[:tq] - lam_ref[0] * o[tq:]
    od = od * lax.rsqrt(jnp.mean(od * od, axis=-1, keepdims=True) + EPS) * g_ref[...]
    o_ref[...] = (od * out_scale).astype(o_ref.dtype)


def _attention(q2, kt_ctx, kt_lat, v, lam, subln_g, lam_init):
    r = v.shape[0]
    tq = ATT_TQ
    assert (SEQ // ATT_TK) % ATT_UNROLL == 0 and ATT_UNROLL % 2 == 0
    return pl.pallas_call(
        functools.partial(_attn_kernel, out_scale=1.0 - lam_init),
        grid=(N_HEADS, r // tq),
        in_specs=[pl.BlockSpec(memory_space=pltpu.SMEM),
                  pl.BlockSpec((2, tq, V_DIM), lambda h, i: (0, i, h)),
                  pl.BlockSpec((1, V_DIM, CTX_LEN), lambda h, i: (h, 0, 0)),
                  pl.BlockSpec((1, SEQ // ATT_TK, V_DIM, ATT_TK), lambda h, i: (h, 0, 0, 0)),
                  pl.BlockSpec((r, 2 * V_DIM), lambda h, i: (0, h)),
                  pl.BlockSpec((1, V_DIM), lambda h, i: (0, 0))],
        out_specs=pl.BlockSpec((tq, V_DIM), lambda h, i: (i, h)),
        out_shape=jax.ShapeDtypeStruct((r, ATTN_WIDTH), BF16),
        scratch_shapes=[pltpu.VMEM((2 * tq, V_DIM), F32), pltpu.VMEM((2 * tq, 2 * V_DIM), F32),
                        pltpu.VMEM((2, 2 * tq, ATT_TK), F32)],
        compiler_params=_cparams("parallel", "arbitrary"),
        name="diff_attention",
    )(lam.reshape(1), q2, kt_ctx, kt_lat, v, subln_g.reshape(1, V_DIM))


def _diff_attention(z, cos_t, sin_t, lq1, lk1, lq2, lk2, subln_g, lam_init):
    lam = (jnp.exp(jnp.sum(lq1.astype(F32) * lk1.astype(F32)))
           - jnp.exp(jnp.sum(lq2.astype(F32) * lk2.astype(F32))) + lam_init)
    q2, k_r, v = _rope(z, cos_t, sin_t)
    kt_ctx = k_r[:CTX_LEN].reshape(CTX_LEN, N_HEADS, V_DIM).transpose(1, 2, 0)
    kt_lat = k_r[CTX_LEN:].reshape(SEQ // ATT_TK, ATT_TK, N_HEADS, V_DIM).transpose(2, 0, 3, 1)
    return _attention(q2, kt_ctx, kt_lat, v, lam.astype(F32), subln_g.astype(F32), lam_init)


def _modulation_tables(s_pair, mod_down, mod_up, mod_b):
    d = D_MODEL
    low = _matmul(s_pair, mod_down, tm=BF16_SUBLANES, tn=mod_down.shape[1], tk=d, out_dtype=F32, name="mod_down")
    m = _matmul(low, mod_up, tm=BF16_SUBLANES, tn=2048, tk=mod_up.shape[0], out_dtype=F32, name="mod_up")
    m = (m[:2] + mod_b[None]).reshape(2, 6, d)
    return jnp.pad(m, ((0, 0), (0, 2), (0, 0)))


def kernel(x, c, ctx, c_ctx, mod_down, mod_up, mod_b, norm_g, w_in, conv_dw, conv_dw_b, conv_ln_g, conv_ln_b, conv_pw, conv_pw_b, ssm_a_re, ssm_a_im, ssm_log_step, ssm_b_re, ssm_b_im, ssm_c_re, ssm_c_im, ssm_d, ssm_glu_w, ssm_glu_b, lam_q1, lam_k1, lam_q2, lam_k2, attn_subln_g, w_out, mlp_w1, mlp_w2):
    d = D_MODEL
    xs = jnp.concatenate([ctx[0], x[0]], axis=0).astype(F32)
    s_pair = jnp.concatenate([jax.nn.silu(c_ctx)[None], jax.nn.silu(c)], axis=0).astype(F32)
    s_pair = jnp.pad(s_pair, ((0, BF16_SUBLANES - 2), (0, 0)))
    cos_t, sin_t = _rope_tables()

    mods = [_modulation_tables(s_pair, mod_down[l], mod_up[l], mod_b[l]) for l in range(DEPTH)]
    h = _normmod(xs, norm_g[0, 0], mods[0], shift_row=0, scale_row=1)

    for l in range(DEPTH):
        lam_init = 0.8 - 0.6 * math.exp(-0.3 * l)
        mod = mods[l]

        z = _matmul(h, w_in[l].astype(BF16), tm=MM_TM, tn=512, tk=d, out_dtype=F32, name="w_in")
        y_conv = _conv_mixer(z, conv_dw[l], conv_dw_b[l], conv_ln_g[l], conv_ln_b[l],
                             conv_pw[l].astype(BF16), conv_pw_b[l])
        ops = _ssm_operators(ssm_a_re[l], ssm_a_im[l], ssm_log_step[l], ssm_b_re[l], ssm_b_im[l],
                             ssm_c_re[l], ssm_c_im[l])
        y_ssm = _s5_mixer(z, ops, ssm_d[l], ssm_glu_w[l].astype(BF16), ssm_glu_b[l])
        y_att = _diff_attention(z, cos_t, sin_t, lam_q1[l], lam_k1[l], lam_q2[l], lam_k2[l],
                                attn_subln_g[l], lam_init)
        ycat = jnp.concatenate([y_conv, y_ssm, y_att], axis=-1)
        y = _matmul(ycat, w_out[l].astype(BF16), tm=MM_TM, tn=512, tk=d, out_dtype=F32, name="w_out")
        xs, h = _resid(y, xs, norm_g[l, 1], norm_g[l, 2], mod, mod,
                       gate_row=2, shift_row=3, scale_row=4, emit_h=True)

        a = _matmul(h, mlp_w1[l].astype(BF16), tm=MM_TM, tn=512, tk=d, out_dtype=BF16, act="sqrelu",
                    name="mlp_w1")
        y = _matmul(a, mlp_w2[l].astype(BF16), tm=MM_TM, tn=1024, tk=2048, out_dtype=F32, name="mlp_w2")
        last = l == DEPTH - 1
        nxt = min(l + 1, DEPTH - 1)
        xs, h = _resid(y, xs, norm_g[l, 3], norm_g[nxt, 0], mod, mods[nxt],
                       gate_row=5, shift_row=0, scale_row=1, emit_h=not last)

    return xs[CTX_LEN:][None].astype(x.dtype)
```

```python
import functools
import math

import jax
import jax.numpy as jnp
from jax import lax
from jax.experimental import pallas as pl
from jax.experimental.pallas import tpu as pltpu

F32 = jnp.float32
BF16 = jnp.bfloat16

D_MODEL = 4096
SEQ = 8192
DEPTH = 4
GRID_W = 64
CTX_LEN = 256
ROWS = CTX_LEN + SEQ
CONV_WIDTH = 1024
SSM_WIDTH = 1024
ATTN_WIDTH = 2048
CONV_KERNEL = 31
CONV_HALF = CONV_KERNEL // 2
SSM_CH = 16
SSM_GROUPS = 64
SSM_STATE = 64
V_DIM = 128
N_HEADS = 16
QK_DIM = 64
ROPE_AXIS_DIM = 32
ROPE_BASE = 10000.0
D_FF = 4 * D_MODEL
EPS = 1e-6
IN_WIDTH = 2 * CONV_WIDTH + SSM_WIDTH + 3 * ATTN_WIDTH

LANES = 128
BF16_SUBLANES = 16
VMEM_LIMIT = 56 * 1024 * 1024

ROW_TILE = 128
MM_TM = 1056
HALO = 16
CONV_ROWS = 32
SSM_T = 8
SSM_GB = LANES // SSM_CH
ATT_TQ = 256
ATT_TK = 512
ATT_UNROLL = 8


def _cparams(*sem):
    return pltpu.CompilerParams(dimension_semantics=sem, vmem_limit_bytes=VMEM_LIMIT)


def _mm_kernel(a_ref, b_ref, o_ref, *scratch, nk, act):
    def finish(acc):
        if act == "sqrelu":
            acc = jnp.square(jnp.maximum(acc, 0.0))
        o_ref[...] = acc.astype(o_ref.dtype)

    prod = jnp.dot(a_ref[...].astype(BF16), b_ref[...].astype(BF16), preferred_element_type=F32)
    if nk == 1:
        finish(prod)
        return
    acc_ref, = scratch
    k = pl.program_id(2)

    @pl.when(k == 0)
    def _():
        acc_ref[...] = prod

    @pl.when(k > 0)
    def _():
        acc_ref[...] += prod

    @pl.when(k == nk - 1)
    def _():
        finish(acc_ref[...])


def _matmul(a, b, *, tm, tn, tk, out_dtype, act=None, name):
    m, kdim = a.shape
    _, n = b.shape
    assert m % tm == 0 and n % tn == 0 and kdim % tk == 0
    nk = kdim // tk
    scratch = [pltpu.VMEM((tm, tn), F32)] if nk > 1 else []
    return pl.pallas_call(
        functools.partial(_mm_kernel, nk=nk, act=act),
        grid=(m // tm, n // tn, nk),
        in_specs=[pl.BlockSpec((tm, tk), lambda i, j, k: (i, k)),
                  pl.BlockSpec((tk, tn), lambda i, j, k: (k, j))],
        out_specs=pl.BlockSpec((tm, tn), lambda i, j, k: (i, j)),
        out_shape=jax.ShapeDtypeStruct((m, n), out_dtype),
        scratch_shapes=scratch,
        compiler_params=_cparams("parallel", "parallel", "arbitrary"),
        name=name,
    )(a, b)


def _mod_index(i):
    return jnp.minimum(i, 1) if CTX_LEN == ROW_TILE else (i >= CTX_LEN // ROW_TILE).astype(jnp.int32)


def _rms(x, g):
    return x * lax.rsqrt(jnp.mean(x * x, axis=-1, keepdims=True) + EPS) * g


def _normmod_kernel(x_ref, g_ref, mod_ref, h_ref, *, shift_row, scale_row):
    mod = mod_ref[0]
    y = _rms(x_ref[...], g_ref[...])
    h = y * (1.0 + mod[scale_row:scale_row + 1]) + mod[shift_row:shift_row + 1]
    h_ref[...] = h.astype(h_ref.dtype)


def _normmod(x, g, mod, *, shift_row, scale_row):
    r, d = x.shape
    return pl.pallas_call(
        functools.partial(_normmod_kernel, shift_row=shift_row, scale_row=scale_row),
        grid=(r // ROW_TILE,),
        in_specs=[pl.BlockSpec((ROW_TILE, d), lambda i: (i, 0)),
                  pl.BlockSpec((1, d), lambda i: (0, 0)),
                  pl.BlockSpec((1, 8, d), lambda i: (_mod_index(i), 0, 0))],
        out_specs=pl.BlockSpec((ROW_TILE, d), lambda i: (i, 0)),
        out_shape=jax.ShapeDtypeStruct((r, d), BF16),
        compiler_params=_cparams("parallel"),
        name="normmod",
    )(x, g.reshape(1, d), mod)


def _resid_kernel(y_ref, x_ref, ga_ref, gb_ref, moda_ref, modb_ref, xo_ref, *h_refs,
                  gate_row, shift_row, scale_row):
    gate = moda_ref[0][gate_row:gate_row + 1]
    xn = x_ref[...] + gate * _rms(y_ref[...], ga_ref[...])
    xo_ref[...] = xn
    if h_refs:
        modb = modb_ref[0]
        h = _rms(xn, gb_ref[...]) * (1.0 + modb[scale_row:scale_row + 1]) + modb[shift_row:shift_row + 1]
        h_refs[0][...] = h.astype(h_refs[0].dtype)


def _resid(y, x, ga, gb, moda, modb, *, gate_row, shift_row, scale_row, emit_h):
    r, d = x.shape
    row = pl.BlockSpec((ROW_TILE, d), lambda i: (i, 0))
    vec = pl.BlockSpec((1, d), lambda i: (0, 0))
    modspec = pl.BlockSpec((1, 8, d), lambda i: (_mod_index(i), 0, 0))
    out_shape = [jax.ShapeDtypeStruct((r, d), F32)]
    out_specs = [row]
    if emit_h:
        out_shape.append(jax.ShapeDtypeStruct((r, d), BF16))
        out_specs.append(row)
    outs = pl.pallas_call(
        functools.partial(_resid_kernel, gate_row=gate_row, shift_row=shift_row, scale_row=scale_row),
        grid=(r // ROW_TILE,),
        in_specs=[row, row, vec, vec, modspec, modspec],
        out_specs=out_specs,
        out_shape=out_shape,
        compiler_params=_cparams("parallel"),
        name="resid",
    )(y, x, ga.reshape(1, d), gb.reshape(1, d), moda, modb)
    return outs if emit_h else (outs[0], None)


def _conv_kernel(ap_ref, ac_ref, an_ref, gp_ref, gc_ref, gn_ref, dw_ref, dwb_ref, lng_ref, lnb_ref,
                 pw_ref, pwb_ref, o_ref, ub_ref, cb_ref, *, nblk, ctx_blocks):
    t = ROW_TILE
    i = pl.program_id(0)
    prev_ok = jnp.logical_and(i != 0, i != ctx_blocks)
    next_ok = jnp.logical_and(i != ctx_blocks - 1, i != nblk - 1)

    def glu(a_ref, g_ref):
        return a_ref[...] * jax.nn.sigmoid(g_ref[...])

    ub_ref[0:HALO, :] = jnp.where(prev_ok, glu(ap_ref, gp_ref), 0.0)
    ub_ref[HALO:HALO + t, :] = glu(ac_ref, gc_ref)
    ub_ref[HALO + t:2 * HALO + t, :] = jnp.where(next_ok, glu(an_ref, gn_ref), 0.0)

    base = HALO - CONV_HALF
    for c in range(CONV_WIDTH // LANES):
        cs = slice(c * LANES, (c + 1) * LANES)
        bias = jnp.broadcast_to(dwb_ref[:, cs], (CONV_ROWS, LANES))
        accs = [bias for _ in range(t // CONV_ROWS)]
        for k in range(CONV_KERNEL):
            wk = jnp.broadcast_to(dw_ref[k:k + 1, cs], (CONV_ROWS, LANES))
            for r in range(t // CONV_ROWS):
                lo = r * CONV_ROWS + base + k
                accs[r] = accs[r] + wk * ub_ref[lo:lo + CONV_ROWS, cs]
        for r in range(t // CONV_ROWS):
            cb_ref[r * CONV_ROWS:(r + 1) * CONV_ROWS, cs] = accs[r]

    cv = cb_ref[...]
    mu = jnp.mean(cv, axis=-1, keepdims=True)
    var = jnp.mean(jnp.square(cv - mu), axis=-1, keepdims=True)
    y = (cv - mu) * lax.rsqrt(var + EPS) * lng_ref[...] + lnb_ref[...]
    y = y * jax.nn.sigmoid(y)
    out = jnp.dot(y.astype(BF16), pw_ref[...], preferred_element_type=F32) + pwb_ref[...]
    o_ref[...] = out.astype(o_ref.dtype)


def _conv_mixer(z, dw, dw_b, ln_g, ln_b, pw_bf16, pw_b):
    r = z.shape[0]
    t, cw = ROW_TILE, CONV_WIDTH
    nblk = r // t
    hb = t // HALO
    nhb = r // HALO

    def cur(col):
        return pl.BlockSpec((t, cw), lambda i: (i, col))

    def prev(col):
        return pl.BlockSpec((HALO, cw), lambda i: (jnp.maximum(i * hb - 1, 0), col))

    def nxt(col):
        return pl.BlockSpec((HALO, cw), lambda i: (jnp.minimum((i + 1) * hb, nhb - 1), col))

    def whole(shape):
        return pl.BlockSpec(shape, lambda i: (0,) * len(shape))

    return pl.pallas_call(
        functools.partial(_conv_kernel, nblk=nblk, ctx_blocks=CTX_LEN // t),
        grid=(nblk,),
        in_specs=[prev(0), cur(0), nxt(0), prev(1), cur(1), nxt(1),
                  whole((CONV_KERNEL, cw)), whole((1, cw)), whole((1, cw)), whole((1, cw)),
                  whole((cw, cw)), whole((1, cw))],
        out_specs=pl.BlockSpec((t, cw), lambda i: (i, 0)),
        out_shape=jax.ShapeDtypeStruct((r, cw), BF16),
        scratch_shapes=[pltpu.VMEM((t + 2 * HALO, cw), F32), pltpu.VMEM((t, cw), F32)],
        compiler_params=_cparams("parallel"),
        name="conv_mixer",
    )(z, z, z, z, z, z, dw, dw_b.reshape(1, cw), ln_g.reshape(1, cw), ln_b.reshape(1, cw),
      pw_bf16, pw_b.reshape(1, cw))


def _ssm_operators(a_re, a_im, log_step, b_re, b_im, c_re, c_im):
    t = SSM_T
    hp = lax.Precision.HIGHEST
    a_re = a_re.astype(F32)
    a_im = a_im.astype(F32)
    dt = jnp.exp(log_step.astype(F32))[..., None]
    mag = jnp.exp(a_re * dt)
    lb_re = mag * jnp.cos(a_im * dt)
    lb_im = mag * jnp.sin(a_im * dt)
    nr, ni = lb_re - 1.0, lb_im
    den = a_re * a_re + a_im * a_im
    f_re = ((nr * a_re + ni * a_im) / den)[..., None]
    f_im = ((ni * a_re - nr * a_im) / den)[..., None]
    b_re = b_re.astype(F32)
    b_im = b_im.astype(F32)
    bb_re = f_re * b_re - f_im * b_im
    bb_im = f_re * b_im + f_im * b_re
    c_re = c_re.astype(F32)
    c_im = c_im.astype(F32)

    k = jnp.arange(t + 1, dtype=F32)[:, None, None, None]
    pmag = jnp.exp(k * (a_re * dt)[None])
    e_re = pmag * jnp.cos(k * (a_im * dt)[None])
    e_im = pmag * jnp.sin(k * (a_im * dt)[None])

    ce_re = c_re[None] * e_re[:, :, :, None, :] - c_im[None] * e_im[:, :, :, None, :]
    ce_im = c_re[None] * e_im[:, :, :, None, :] + c_im[None] * e_re[:, :, :, None, :]
    kk = (jnp.einsum("kdghp,dgpj->kdghj", ce_re, bb_re, precision=hp)
          - jnp.einsum("kdghp,dgpj->kdghj", ce_im, bb_im, precision=hp))

    nb, gl, hh, pp = SSM_GROUPS // SSM_GB, SSM_GB, SSM_CH, SSM_STATE
    eye = jnp.eye(gl, dtype=F32)

    lag = jnp.arange(t)[None, :] - jnp.arange(t)[:, None]
    sel = (jnp.where((lag >= 0)[:, :, None, None, None], kk[:, 0][jnp.clip(lag, 0, t)], 0.0)
           + jnp.where((lag <= 0)[:, :, None, None, None], kk[:, 1][jnp.clip(-lag, 0, t)], 0.0))
    sel = sel.reshape(t, t, nb, gl, hh, hh).transpose(2, 0, 3, 5, 1, 4)
    mt = (sel[:, :, :, :, :, None, :] * eye[None, None, :, None, None, :, None]).reshape(nb, t * LANES, t * LANES)

    er = jnp.stack([e_re[t - 1::-1][:t][:, 0], e_re[:t][:, 1]], axis=1)
    ei = jnp.stack([e_im[t - 1::-1][:t][:, 0], e_im[:t][:, 1]], axis=1)
    w_re = er[..., None] * bb_re[None] - ei[..., None] * bb_im[None]
    w_im = er[..., None] * bb_im[None] + ei[..., None] * bb_re[None]
    w = jnp.stack([w_re, w_im], axis=0).reshape(2, t, 2, nb, gl, pp, hh)
    w = w.transpose(2, 3, 1, 4, 6, 0, 5)
    w = (w[:, :, :, :, :, :, None, :] * eye[None, None, None, :, None, None, :, None])
    w = w.reshape(2, nb, t * LANES, 2 * gl * pp)

    cr = jnp.stack([ce_re[1:t + 1][:, 0], ce_re[t:0:-1][:, 1]], axis=1)
    ci = jnp.stack([ce_im[1:t + 1][:, 0], ce_im[t:0:-1][:, 1]], axis=1)
    v = jnp.stack([cr, -ci], axis=0).reshape(2, t, 2, nb, gl, hh, pp)
    v = v.transpose(2, 3, 0, 4, 6, 1, 5)
    v = (v[:, :, :, :, :, :, None, :] * eye[None, None, None, :, None, None, :, None])
    v = v.reshape(2, nb, 2 * gl * pp, t * LANES)

    decay = jnp.stack([e_re[t].reshape(2, nb, gl * pp), e_im[t].reshape(2, nb, gl * pp)], axis=2)
    return mt.astype(BF16), w.astype(BF16), v.astype(BF16), decay


def _ssm_kernel(u_ref, mt_ref, w_ref, v_ref, decay_ref, y_ref, u2_ref, zs_ref, y2_ref):
    t = SSM_T
    nch = u2_ref.shape[0]
    nctx = CTX_LEN // t
    half = SSM_GB * SSM_STATE
    d = pl.program_id(1)

    @pl.when(d == 0)
    def _():
        for k in range(t):
            u2_ref[:, k * LANES:(k + 1) * LANES] = u_ref[pl.ds(k, nch, stride=t), :].astype(BF16)

    u2 = u2_ref[...]
    zs_ref[...] = jnp.dot(u2, w_ref[0, 0], preferred_element_type=F32)

    lr = decay_ref[0, 0, 0:1, :]
    li = decay_ref[0, 0, 1:2, :]

    def step(i, carry):
        sr, si = carry
        back = jnp.where(i < nctx, nctx - 1 - i, nch + nctx - 1 - i)
        c = jnp.where(d == 0, i, back)
        zr = zs_ref[pl.ds(c, 1), 0:half]
        zi = zs_ref[pl.ds(c, 1), half:2 * half]
        zs_ref[pl.ds(c, 1), 0:half] = sr
        zs_ref[pl.ds(c, 1), half:2 * half] = si
        return lr * sr - li * si + zr, lr * si + li * sr + zi

    zero = jnp.zeros((1, half), F32)
    lax.fori_loop(0, nch, step, (zero, zero))

    carried = jnp.dot(zs_ref[...].astype(BF16), v_ref[0, 0], preferred_element_type=F32)

    @pl.when(d == 0)
    def _():
        y2_ref[...] = jnp.dot(u2, mt_ref[0], preferred_element_type=F32) + carried

    @pl.when(d == 1)
    def _():
        y2_ref[...] += carried
        for k in range(t):
            y_ref[pl.ds(k, nch, stride=t), :] = y2_ref[:, k * LANES:(k + 1) * LANES]


def _ssm_core(z, ops):
    mt, w, v, decay = ops
    r = z.shape[0]
    t = SSM_T
    nch = r // t
    nb = SSM_WIDTH // LANES
    tl = t * LANES
    ns = 2 * SSM_GB * SSM_STATE
    ucol = 2 * CONV_WIDTH // LANES
    return pl.pallas_call(
        _ssm_kernel,
        grid=(nb, 2),
        in_specs=[pl.BlockSpec((r, LANES), lambda b, d: (0, ucol + b)),
                  pl.BlockSpec((1, tl, tl), lambda b, d: (b, 0, 0)),
                  pl.BlockSpec((1, 1, tl, ns), lambda b, d: (d, b, 0, 0)),
                  pl.BlockSpec((1, 1, ns, tl), lambda b, d: (d, b, 0, 0)),
                  pl.BlockSpec((1, 1, 2, ns // 2), lambda b, d: (d, b, 0, 0))],
        out_specs=pl.BlockSpec((r, LANES), lambda b, d: (0, b)),
        out_shape=jax.ShapeDtypeStruct((r, SSM_WIDTH), F32),
        scratch_shapes=[pltpu.VMEM((nch, tl), BF16), pltpu.VMEM((nch, ns), F32), pltpu.VMEM((nch, tl), F32)],
        compiler_params=_cparams("parallel", "arbitrary"),
        name="ssm_core",
    )(z, mt, w, v, decay)


def _ssm_glu_kernel(u_ref, y_ref, d_ref, w_ref, b_ref, o_ref):
    y = d_ref[...] * u_ref[...] + y_ref[...]
    g = jax.nn.gelu(y)
    gate = jnp.dot(g.astype(BF16), w_ref[...], preferred_element_type=F32) + b_ref[...]
    o_ref[...] = (g * jax.nn.sigmoid(gate)).astype(o_ref.dtype)


def _ssm_glu(z, y, d, w_bf16, b):
    r = z.shape[0]
    t, sw = ROW_TILE, SSM_WIDTH
    ucol = 2 * CONV_WIDTH // sw
    row = pl.BlockSpec((t, sw), lambda i: (i, 0))
    vec = pl.BlockSpec((1, sw), lambda i: (0, 0))
    return pl.pallas_call(
        _ssm_glu_kernel,
        grid=(r // t,),
        in_specs=[pl.BlockSpec((t, sw), lambda i: (i, ucol)), row, vec,
                  pl.BlockSpec((sw, sw), lambda i: (0, 0)), vec],
        out_specs=row,
        out_shape=jax.ShapeDtypeStruct((r, sw), BF16),
        compiler_params=_cparams("parallel"),
        name="ssm_glu",
    )(z, y, d.reshape(1, sw), w_bf16, b.reshape(1, sw))


def _s5_mixer(z, ops, d, glu_w_bf16, glu_b):
    return _ssm_glu(z, _ssm_core(z, ops), d, glu_w_bf16, glu_b)


def _rope_tables():
    rows = SEQ // GRID_W
    row = jnp.repeat(jnp.arange(rows, dtype=F32), GRID_W)
    col = jnp.tile(jnp.arange(GRID_W, dtype=F32), rows)
    inv = jnp.power(ROPE_BASE, -jnp.arange(0, ROPE_AXIS_DIM, 2, dtype=F32) / ROPE_AXIS_DIM)
    ang_r = row[:, None] * inv[None]
    ang_c = col[:, None] * inv[None]
    cos_map = jnp.concatenate([jnp.cos(ang_r), jnp.cos(ang_r), jnp.cos(ang_c), jnp.cos(ang_c)], axis=-1)
    sin_map = jnp.concatenate([-jnp.sin(ang_r), jnp.sin(ang_r), -jnp.sin(ang_c), jnp.sin(ang_c)], axis=-1)
    cos_t = jnp.concatenate([cos_map, cos_map], axis=-1)
    sin_t = jnp.concatenate([sin_map, sin_map], axis=-1)
    cos_t = jnp.concatenate([jnp.ones((CTX_LEN, V_DIM), F32), cos_t], axis=0)
    sin_t = jnp.concatenate([jnp.zeros((CTX_LEN, V_DIM), F32), sin_t], axis=0)
    return cos_t, sin_t


def _rope_kernel(zq_ref, zk_ref, zv_ref, cos_ref, sin_ref, q_ref, k_ref, v_ref):
    cos = cos_ref[...]
    sin = sin_ref[...]
    lane = lax.broadcasted_iota(jnp.int32, cos.shape, 1)
    first_of_pair = (lane & (ROPE_AXIS_DIM // 2)) == 0
    map0 = lane < QK_DIM
    half = ROPE_AXIS_DIM // 2

    def rot(x):
        partner = jnp.where(first_of_pair, pltpu.roll(x, V_DIM - half, axis=1), pltpu.roll(x, half, axis=1))
        return x * cos + partner * sin

    for h in range(zq_ref.shape[1] // V_DIM):
        hs = slice(h * V_DIM, (h + 1) * V_DIM)
        q = rot(zq_ref[:, hs]) * (QK_DIM ** -0.5)
        q_ref[0, :, hs] = jnp.where(map0, q, 0.0).astype(q_ref.dtype)
        q_ref[1, :, hs] = jnp.where(map0, 0.0, q).astype(q_ref.dtype)
        k_ref[:, hs] = rot(zk_ref[:, hs]).astype(k_ref.dtype)
    ones = jnp.ones((zv_ref.shape[0], V_DIM), v_ref.dtype)
    for h in range(zv_ref.shape[1] // V_DIM):
        v_ref[:, 2 * h * V_DIM:(2 * h + 1) * V_DIM] = zv_ref[:, h * V_DIM:(h + 1) * V_DIM].astype(v_ref.dtype)
        v_ref[:, (2 * h + 1) * V_DIM:(2 * h + 2) * V_DIM] = ones


def _rope(z, cos_t, sin_t):
    r = z.shape[0]
    t, w = ROW_TILE, 1024
    qc = (2 * CONV_WIDTH + SSM_WIDTH) // w
    kc = qc + ATTN_WIDTH // w
    vc = kc + ATTN_WIDTH // w
    tab = pl.BlockSpec((t, V_DIM), lambda i, j: (i, 0))
    return pl.pallas_call(
        _rope_kernel,
        grid=(r // t, ATTN_WIDTH // w),
        in_specs=[pl.BlockSpec((t, w), lambda i, j: (i, qc + j)),
                  pl.BlockSpec((t, w), lambda i, j: (i, kc + j)),
                  pl.BlockSpec((t, w), lambda i, j: (i, vc + j)), tab, tab],
        out_specs=[pl.BlockSpec((2, t, w), lambda i, j: (0, i, j)),
                   pl.BlockSpec((t, w), lambda i, j: (i, j)),
                   pl.BlockSpec((t, 2 * w), lambda i, j: (i, j))],
        out_shape=[jax.ShapeDtypeStruct((2, r, ATTN_WIDTH), BF16),
                   jax.ShapeDtypeStruct((r, ATTN_WIDTH), BF16),
                   jax.ShapeDtypeStruct((r, 2 * ATTN_WIDTH), BF16)],
        compiler_params=_cparams("parallel", "parallel"),
        name="rope",
    )(z, z, z, cos_t, sin_t)


def _attn_kernel(lam_ref, q_ref, k_ref, v_ref, g_ref, o_ref, m_ref, acc_ref, s_ref, *, out_scale):
    tq = ATT_TQ
    iq = pl.program_id(1)
    q = q_ref[...].reshape(2 * tq, V_DIM)

    def scores(k):
        return lax.dot_general(q, k, (((1,), (1,)), ((), ())), preferred_element_type=F32)

    def krows(j):
        return k_ref[pl.ds(pl.multiple_of(CTX_LEN + j * ATT_TK, ATT_TK // 2), ATT_TK), :]

    def lane_tiles(s):
        return [s[:, c * LANES:(c + 1) * LANES] for c in range(s.shape[1] // LANES)]

    def consume_first(s, v):
        tiles = lane_tiles(s)
        m_new = jnp.broadcast_to(jnp.max(functools.reduce(jnp.maximum, tiles), axis=-1, keepdims=True),
                                 m_ref.shape)
        p = jnp.concatenate([jnp.exp(t - m_new) for t in tiles], axis=-1).astype(BF16)
        acc_ref[...] = jnp.dot(p, v, preferred_element_type=F32)
        m_ref[...] = m_new

    def consume(s, v):
        tiles = lane_tiles(s)
        m_old = m_ref[...]
        m_new = jnp.maximum(m_old, jnp.max(functools.reduce(jnp.maximum, tiles), axis=-1, keepdims=True))
        alpha = jnp.exp(m_old - m_new)
        p = jnp.concatenate([jnp.exp(t - m_new) for t in tiles], axis=-1).astype(BF16)
        pv = jnp.dot(p, v, preferred_element_type=F32)
        acc_ref[...] = jnp.concatenate([alpha, alpha], axis=-1) * acc_ref[...] + pv
        m_ref[...] = m_new

    def vrows(j):
        return v_ref[pl.ds(pl.multiple_of(CTX_LEN + j * ATT_TK, ATT_TK // 2), ATT_TK), :]

    nlc = SEQ // ATT_TK
    is_ctx = iq < CTX_LEN // tq

    @pl.when(is_ctx)
    def _():
        consume_first(scores(k_ref[0:CTX_LEN, :]), v_ref[0:CTX_LEN, :])

    @pl.when(jnp.logical_not(is_ctx))
    def _():
        s_ref[0] = scores(krows(0))
        consume_first(scores(k_ref[0:CTX_LEN, :]), v_ref[0:CTX_LEN, :])

        def trip(i, carry):
            j = ATT_UNROLL * i
            for u in range(ATT_UNROLL):
                s_ref[(u + 1) % 2] = scores(krows(jnp.minimum(j + u + 1, nlc - 1)))
                consume(s_ref[u % 2], vrows(j + u))
            return carry

        lax.fori_loop(0, nlc // ATT_UNROLL, trip, 0)

    o = acc_ref[:, :V_DIM] / acc_ref[:, V_DIM:]
    od = o[:tq] - lam_ref[0] * o[tq:]
    od = od * lax.rsqrt(jnp.mean(od * od, axis=-1, keepdims=True) + EPS) * g_ref[...]
    o_ref[...] = (od * out_scale).astype(o_ref.dtype)


def _attention(q2, k_r, v, lam, subln_g, lam_init):
    r = v.shape[0]
    tq = ATT_TQ
    assert (SEQ // ATT_TK) % ATT_UNROLL == 0 and ATT_UNROLL % 2 == 0
    return pl.pallas_call(
        functools.partial(_attn_kernel, out_scale=1.0 - lam_init),
        grid=(N_HEADS, r // tq),
        in_specs=[pl.BlockSpec(memory_space=pltpu.SMEM),
                  pl.BlockSpec((2, tq, V_DIM), lambda h, i: (0, i, h)),
                  pl.BlockSpec((r, V_DIM), lambda h, i: (0, h)),
                  pl.BlockSpec((r, 2 * V_DIM), lambda h, i: (0, h)),
                  pl.BlockSpec((1, V_DIM), lambda h, i: (0, 0))],
        out_specs=pl.BlockSpec((tq, V_DIM), lambda h, i: (i, h)),
        out_shape=jax.ShapeDtypeStruct((r, ATTN_WIDTH), BF16),
        scratch_shapes=[pltpu.VMEM((2 * tq, V_DIM), F32), pltpu.VMEM((2 * tq, 2 * V_DIM), F32),
                        pltpu.VMEM((2, 2 * tq, ATT_TK), F32)],
        compiler_params=_cparams("parallel", "arbitrary"),
        name="diff_attention",
    )(lam.reshape(1), q2, k_r, v, subln_g.reshape(1, V_DIM))


def _diff_attention(z, cos_t, sin_t, lq1, lk1, lq2, lk2, subln_g, lam_init):
    lam = (jnp.exp(jnp.sum(lq1.astype(F32) * lk1.astype(F32)))
           - jnp.exp(jnp.sum(lq2.astype(F32) * lk2.astype(F32))) + lam_init)
    q2, k_r, v = _rope(z, cos_t, sin_t)
    return _attention(q2, k_r, v, lam.astype(F32), subln_g.astype(F32), lam_init)


def _modulation_tables(s_pair, mod_down, mod_up, mod_b):
    d = D_MODEL
    low = _matmul(s_pair, mod_down, tm=BF16_SUBLANES, tn=mod_down.shape[1], tk=d, out_dtype=F32, name="mod_down")
    m = _matmul(low, mod_up, tm=BF16_SUBLANES, tn=2048, tk=mod_up.shape[0], out_dtype=F32, name="mod_up")
    m = (m[:2] + mod_b[None]).reshape(2, 6, d)
    return jnp.pad(m, ((0, 0), (0, 2), (0, 0)))


def kernel(x, c, ctx, c_ctx, mod_down, mod_up, mod_b, norm_g, w_in, conv_dw, conv_dw_b, conv_ln_g, conv_ln_b, conv_pw, conv_pw_b, ssm_a_re, ssm_a_im, ssm_log_step, ssm_b_re, ssm_b_im, ssm_c_re, ssm_c_im, ssm_d, ssm_glu_w, ssm_glu_b, lam_q1, lam_k1, lam_q2, lam_k2, attn_subln_g, w_out, mlp_w1, mlp_w2):
    d = D_MODEL
    xs = jnp.concatenate([ctx[0], x[0]], axis=0).astype(F32)
    s_pair = jnp.concatenate([jax.nn.silu(c_ctx)[None], jax.nn.silu(c)], axis=0).astype(F32)
    s_pair = jnp.pad(s_pair, ((0, BF16_SUBLANES - 2), (0, 0)))
    cos_t, sin_t = _rope_tables()

    mods = [_modulation_tables(s_pair, mod_down[l], mod_up[l], mod_b[l]) for l in range(DEPTH)]
    h = _normmod(xs, norm_g[0, 0], mods[0], shift_row=0, scale_row=1)

    for l in range(DEPTH):
        lam_init = 0.8 - 0.6 * math.exp(-0.3 * l)
        mod = mods[l]

        z = _matmul(h, w_in[l].astype(BF16), tm=MM_TM, tn=512, tk=d, out_dtype=F32, name="w_in")
        y_conv = _conv_mixer(z, conv_dw[l], conv_dw_b[l], conv_ln_g[l], conv_ln_b[l],
                             conv_pw[l].astype(BF16), conv_pw_b[l])
        ops = _ssm_operators(ssm_a_re[l], ssm_a_im[l], ssm_log_step[l], ssm_b_re[l], ssm_b_im[l],
                             ssm_c_re[l], ssm_c_im[l])
        y_ssm = _s5_mixer(z, ops, ssm_d[l], ssm_glu_w[l].astype(BF16), ssm_glu_b[l])
        y_att = _diff_attention(z, cos_t, sin_t, lam_q1[l], lam_k1[l], lam_q2[l], lam_k2[l],
                                attn_subln_g[l], lam_init)
        ycat = jnp.concatenate([y_conv, y_ssm, y_att], axis=-1)
        y = _matmul(ycat, w_out[l].astype(BF16), tm=MM_TM, tn=512, tk=d, out_dtype=F32, name="w_out")
        xs, h = _resid(y, xs, norm_g[l, 1], norm_g[l, 2], mod, mod,
                       gate_row=2, shift_row=3, scale_row=4, emit_h=True)

        a = _matmul(h, mlp_w1[l].astype(BF16), tm=MM_TM, tn=512, tk=d, out_dtype=BF16, act="sqrelu",
                    name="mlp_w1")
        y = _matmul(a, mlp_w2[l].astype(BF16), tm=MM_TM, tn=1024, tk=2048, out_dtype=F32, name="mlp_w2")
        last = l == DEPTH - 1
        nxt = min(l + 1, DEPTH - 1)
        xs, h = _resid(y, xs, norm_g[l, 3], norm_g[nxt, 0], mod, mods[nxt],
                       gate_row=5, shift_row=0, scale_row=1, emit_h=not last)

    return xs[CTX_LEN:][None].astype(x.dtype)
```

```python
import functools
import math

import jax
import jax.numpy as jnp
from jax import lax
from jax.experimental import pallas as pl
from jax.experimental.pallas import tpu as pltpu

F32 = jnp.float32
BF16 = jnp.bfloat16

D_MODEL = 4096
SEQ = 8192
DEPTH = 4
GRID_W = 64
CTX_LEN = 256
ROWS = CTX_LEN + SEQ
CONV_WIDTH = 1024
SSM_WIDTH = 1024
ATTN_WIDTH = 2048
CONV_KERNEL = 31
CONV_HALF = CONV_KERNEL // 2
SSM_CH = 16
SSM_GROUPS = 64
SSM_STATE = 64
V_DIM = 128
N_HEADS = 16
QK_DIM = 64
ROPE_AXIS_DIM = 32
ROPE_BASE = 10000.0
D_FF = 4 * D_MODEL
EPS = 1e-6
IN_WIDTH = 2 * CONV_WIDTH + SSM_WIDTH + 3 * ATTN_WIDTH

LANES = 128
BF16_SUBLANES = 16
VMEM_LIMIT = 56 * 1024 * 1024

ROW_TILE = 128
MM_TM = 1056
HALO = 16
CONV_ROWS = 32
SSM_T = 8
SSM_GB = LANES // SSM_CH
ATT_TQ = 256
ATT_TK = 512
ATT_UNROLL = 8


def _cparams(*sem):
    return pltpu.CompilerParams(dimension_semantics=sem, vmem_limit_bytes=VMEM_LIMIT)


def _mm_kernel(a_ref, b_ref, o_ref, *scratch, nk, act):
    def finish(acc):
        if act == "sqrelu":
            acc = jnp.square(jnp.maximum(acc, 0.0))
        o_ref[...] = acc.astype(o_ref.dtype)

    prod = jnp.dot(a_ref[...].astype(BF16), b_ref[...].astype(BF16), preferred_element_type=F32)
    if nk == 1:
        finish(prod)
        return
    acc_ref, = scratch
    k = pl.program_id(2)

    @pl.when(k == 0)
    def _():
        acc_ref[...] = prod

    @pl.when(k > 0)
    def _():
        acc_ref[...] += prod

    @pl.when(k == nk - 1)
    def _():
        finish(acc_ref[...])


def _matmul(a, b, *, tm, tn, tk, out_dtype, act=None, name, layer=None):
    m, kdim = a.shape
    n = b.shape[-1]
    assert m % tm == 0 and n % tn == 0 and kdim % tk == 0 and b.shape[-2] == kdim
    nk = kdim // tk
    scratch = [pltpu.VMEM((tm, tn), F32)] if nk > 1 else []
    if layer is None:
        b_spec = pl.BlockSpec((tk, tn), lambda i, j, k: (k, j))
    else:
        b_spec = pl.BlockSpec((None, tk, tn), lambda i, j, k: (layer, k, j))
    return pl.pallas_call(
        functools.partial(_mm_kernel, nk=nk, act=act),
        grid=(m // tm, n // tn, nk),
        in_specs=[pl.BlockSpec((tm, tk), lambda i, j, k: (i, k)), b_spec],
        out_specs=pl.BlockSpec((tm, tn), lambda i, j, k: (i, j)),
        out_shape=jax.ShapeDtypeStruct((m, n), out_dtype),
        scratch_shapes=scratch,
        compiler_params=_cparams("parallel", "parallel", "arbitrary"),
        name=name,
    )(a, b)


def _mod_index(i):
    return jnp.minimum(i, 1) if CTX_LEN == ROW_TILE else (i >= CTX_LEN // ROW_TILE).astype(jnp.int32)


def _rms(x, g):
    return x * lax.rsqrt(jnp.mean(x * x, axis=-1, keepdims=True) + EPS) * g


def _normmod_kernel(x_ref, g_ref, mod_ref, h_ref, *, shift_row, scale_row):
    mod = mod_ref[0]
    y = _rms(x_ref[...], g_ref[...])
    h = y * (1.0 + mod[scale_row:scale_row + 1]) + mod[shift_row:shift_row + 1]
    h_ref[...] = h.astype(h_ref.dtype)


def _normmod(x, g, mod, *, shift_row, scale_row):
    r, d = x.shape
    return pl.pallas_call(
        functools.partial(_normmod_kernel, shift_row=shift_row, scale_row=scale_row),
        grid=(r // ROW_TILE,),
        in_specs=[pl.BlockSpec((ROW_TILE, d), lambda i: (i, 0)),
                  pl.BlockSpec((1, d), lambda i: (0, 0)),
                  pl.BlockSpec((1, 8, d), lambda i: (_mod_index(i), 0, 0))],
        out_specs=pl.BlockSpec((ROW_TILE, d), lambda i: (i, 0)),
        out_shape=jax.ShapeDtypeStruct((r, d), BF16),
        compiler_params=_cparams("parallel"),
        name="normmod",
    )(x, g.reshape(1, d), mod)


def _resid_kernel(y_ref, x_ref, ga_ref, gb_ref, moda_ref, modb_ref, xo_ref, *h_refs,
                  gate_row, shift_row, scale_row):
    gate = moda_ref[0][gate_row:gate_row + 1]
    xn = x_ref[...] + gate * _rms(y_ref[...], ga_ref[...])
    xo_ref[...] = xn
    if h_refs:
        modb = modb_ref[0]
        h = _rms(xn, gb_ref[...]) * (1.0 + modb[scale_row:scale_row + 1]) + modb[shift_row:shift_row + 1]
        h_refs[0][...] = h.astype(h_refs[0].dtype)


def _resid(y, x, ga, gb, moda, modb, *, gate_row, shift_row, scale_row, emit_h):
    r, d = x.shape
    row = pl.BlockSpec((ROW_TILE, d), lambda i: (i, 0))
    vec = pl.BlockSpec((1, d), lambda i: (0, 0))
    modspec = pl.BlockSpec((1, 8, d), lambda i: (_mod_index(i), 0, 0))
    out_shape = [jax.ShapeDtypeStruct((r, d), F32)]
    out_specs = [row]
    if emit_h:
        out_shape.append(jax.ShapeDtypeStruct((r, d), BF16))
        out_specs.append(row)
    outs = pl.pallas_call(
        functools.partial(_resid_kernel, gate_row=gate_row, shift_row=shift_row, scale_row=scale_row),
        grid=(r // ROW_TILE,),
        in_specs=[row, row, vec, vec, modspec, modspec],
        out_specs=out_specs,
        out_shape=out_shape,
        compiler_params=_cparams("parallel"),
        name="resid",
    )(y, x, ga.reshape(1, d), gb.reshape(1, d), moda, modb)
    return outs if emit_h else (outs[0], None)


def _conv_kernel(ap_ref, ac_ref, an_ref, gp_ref, gc_ref, gn_ref, dw_ref, dwb_ref, lng_ref, lnb_ref,
                 pw_ref, pwb_ref, o_ref, ub_ref, cb_ref, *, nblk, ctx_blocks):
    t = ROW_TILE
    i = pl.program_id(0)
    prev_ok = jnp.logical_and(i != 0, i != ctx_blocks)
    next_ok = jnp.logical_and(i != ctx_blocks - 1, i != nblk - 1)

    def glu(a_ref, g_ref):
        return a_ref[...] * jax.nn.sigmoid(g_ref[...])

    ub_ref[0:HALO, :] = jnp.where(prev_ok, glu(ap_ref, gp_ref), 0.0)
    ub_ref[HALO:HALO + t, :] = glu(ac_ref, gc_ref)
    ub_ref[HALO + t:2 * HALO + t, :] = jnp.where(next_ok, glu(an_ref, gn_ref), 0.0)

    base = HALO - CONV_HALF
    for c in range(CONV_WIDTH // LANES):
        cs = slice(c * LANES, (c + 1) * LANES)
        bias = jnp.broadcast_to(dwb_ref[:, cs], (CONV_ROWS, LANES))
        accs = [bias for _ in range(t // CONV_ROWS)]
        for k in range(CONV_KERNEL):
            wk = jnp.broadcast_to(dw_ref[k:k + 1, cs], (CONV_ROWS, LANES))
            for r in range(t // CONV_ROWS):
                lo = r * CONV_ROWS + base + k
                accs[r] = accs[r] + wk * ub_ref[lo:lo + CONV_ROWS, cs]
        for r in range(t // CONV_ROWS):
            cb_ref[r * CONV_ROWS:(r + 1) * CONV_ROWS, cs] = accs[r]

    cv = cb_ref[...]
    mu = jnp.mean(cv, axis=-1, keepdims=True)
    var = jnp.mean(jnp.square(cv - mu), axis=-1, keepdims=True)
    y = (cv - mu) * lax.rsqrt(var + EPS) * lng_ref[...] + lnb_ref[...]
    y = y * jax.nn.sigmoid(y)
    out = jnp.dot(y.astype(BF16), pw_ref[...], preferred_element_type=F32) + pwb_ref[...]
    o_ref[...] = out.astype(o_ref.dtype)


def _conv_mixer(z, dw, dw_b, ln_g, ln_b, pw_bf16, pw_b):
    r = z.shape[0]
    t, cw = ROW_TILE, CONV_WIDTH
    nblk = r // t
    hb = t // HALO
    nhb = r // HALO

    def cur(col):
        return pl.BlockSpec((t, cw), lambda i: (i, col))

    def prev(col):
        return pl.BlockSpec((HALO, cw), lambda i: (jnp.maximum(i * hb - 1, 0), col))

    def nxt(col):
        return pl.BlockSpec((HALO, cw), lambda i: (jnp.minimum((i + 1) * hb, nhb - 1), col))

    def whole(shape):
        return pl.BlockSpec(shape, lambda i: (0,) * len(shape))

    return pl.pallas_call(
        functools.partial(_conv_kernel, nblk=nblk, ctx_blocks=CTX_LEN // t),
        grid=(nblk,),
        in_specs=[prev(0), cur(0), nxt(0), prev(1), cur(1), nxt(1),
                  whole((CONV_KERNEL, cw)), whole((1, cw)), whole((1, cw)), whole((1, cw)),
                  whole((cw, cw)), whole((1, cw))],
        out_specs=pl.BlockSpec((t, cw), lambda i: (i, 0)),
        out_shape=jax.ShapeDtypeStruct((r, cw), BF16),
        scratch_shapes=[pltpu.VMEM((t + 2 * HALO, cw), F32), pltpu.VMEM((t, cw), F32)],
        compiler_params=_cparams("parallel"),
        name="conv_mixer",
    )(z, z, z, z, z, z, dw, dw_b.reshape(1, cw), ln_g.reshape(1, cw), ln_b.reshape(1, cw),
      pw_bf16, pw_b.reshape(1, cw))


def _ssm_operators(a_re, a_im, log_step, b_re, b_im, c_re, c_im):
    t = SSM_T
    hp = lax.Precision.HIGHEST
    a_re = a_re.astype(F32)
    a_im = a_im.astype(F32)
    dt = jnp.exp(log_step.astype(F32))[..., None]
    mag = jnp.exp(a_re * dt)
    lb_re = mag * jnp.cos(a_im * dt)
    lb_im = mag * jnp.sin(a_im * dt)
    nr, ni = lb_re - 1.0, lb_im
    den = a_re * a_re + a_im * a_im
    f_re = ((nr * a_re + ni * a_im) / den)[..., None]
    f_im = ((ni * a_re - nr * a_im) / den)[..., None]
    b_re = b_re.astype(F32)
    b_im = b_im.astype(F32)
    bb_re = f_re * b_re - f_im * b_im
    bb_im = f_re * b_im + f_im * b_re
    c_re = c_re.astype(F32)
    c_im = c_im.astype(F32)

    k = jnp.arange(t + 1, dtype=F32)[:, None, None, None]
    pmag = jnp.exp(k * (a_re * dt)[None])
    e_re = pmag * jnp.cos(k * (a_im * dt)[None])
    e_im = pmag * jnp.sin(k * (a_im * dt)[None])

    ce_re = c_re[None] * e_re[:, :, :, None, :] - c_im[None] * e_im[:, :, :, None, :]
    ce_im = c_re[None] * e_im[:, :, :, None, :] + c_im[None] * e_re[:, :, :, None, :]
    kk = (jnp.einsum("kdghp,dgpj->kdghj", ce_re, bb_re, precision=hp)
          - jnp.einsum("kdghp,dgpj->kdghj", ce_im, bb_im, precision=hp))

    nb, gl, hh, pp = SSM_GROUPS // SSM_GB, SSM_GB, SSM_CH, SSM_STATE

    lag = jnp.arange(t)[None, :] - jnp.arange(t)[:, None]
    sel = (jnp.where((lag >= 0)[:, :, None, None, None], kk[:, 0][jnp.clip(lag, 0, t)], 0.0)
           + jnp.where((lag <= 0)[:, :, None, None, None], kk[:, 1][jnp.clip(-lag, 0, t)], 0.0))
    mtc = sel.reshape(t, t, nb, gl, hh, hh).transpose(2, 0, 3, 5, 1, 4).reshape(nb, t * LANES, t * hh)

    er = jnp.stack([e_re[t - 1::-1][:t][:, 0], e_re[:t][:, 1]], axis=1)
    ei = jnp.stack([e_im[t - 1::-1][:t][:, 0], e_im[:t][:, 1]], axis=1)
    w_re = er[..., None] * bb_re[None] - ei[..., None] * bb_im[None]
    w_im = er[..., None] * bb_im[None] + ei[..., None] * bb_re[None]
    wc = jnp.stack([w_re, w_im], axis=0).reshape(2, t, 2, nb, gl, pp, hh)
    wc = wc.transpose(2, 3, 1, 4, 6, 0, 5).reshape(2, nb, t * LANES, 2 * pp)

    cr = jnp.stack([ce_re[1:t + 1][:, 0], ce_re[t:0:-1][:, 1]], axis=1)
    ci = jnp.stack([ce_im[1:t + 1][:, 0], ce_im[t:0:-1][:, 1]], axis=1)
    vc = jnp.stack([cr, -ci], axis=0).reshape(2, t, 2, nb, gl, hh, pp)
    vc = vc.transpose(2, 3, 0, 4, 6, 1, 5).reshape(2, nb, 2 * gl * pp, t * hh)

    mt = _expand(mtc, row_unit=hh, col_unit=hh)
    w = _expand(wc, row_unit=hh, col_unit=pp)
    v = _expand(vc, row_unit=pp, col_unit=hh)
    decay = jnp.stack([e_re[t].reshape(2, nb, gl * pp), e_im[t].reshape(2, nb, gl * pp)], axis=2)
    return mt, w, v, decay


def _expand(compact, *, row_unit, col_unit):
    gl = SSM_GB
    rows, cin = compact.shape[-2:]
    cout = cin * gl
    c = jnp.arange(cout)
    src = (c // (gl * col_unit)) * col_unit + c % col_unit
    spread = (jnp.arange(cin)[:, None] == src[None, :]).astype(BF16)
    row_g = (jnp.arange(rows) // row_unit) % gl
    col_g = (c // col_unit) % gl
    mask = (row_g[:, None] == col_g[None, :]).astype(BF16)
    wide = jnp.einsum("...rk,kc->...rc", compact.astype(BF16), spread, preferred_element_type=BF16)
    return wide * mask


def _ssm_kernel(u_ref, mt_ref, w_ref, v_ref, decay_ref, y_ref, u2_ref, zs_ref, y2_ref):
    t = SSM_T
    nch = u2_ref.shape[0]
    nctx = CTX_LEN // t
    half = SSM_GB * SSM_STATE
    d = pl.program_id(1)

    @pl.when(d == 0)
    def _():
        for k in range(t):
            u2_ref[:, k * LANES:(k + 1) * LANES] = u_ref[pl.ds(k, nch, stride=t), :].astype(BF16)

    u2 = u2_ref[...]
    zs_ref[...] = jnp.dot(u2, w_ref[0, 0], preferred_element_type=F32)

    lr = decay_ref[0, 0, 0:1, :]
    li = decay_ref[0, 0, 1:2, :]

    def step(i, carry):
        sr, si = carry
        back = jnp.where(i < nctx, nctx - 1 - i, nch + nctx - 1 - i)
        c = jnp.where(d == 0, i, back)
        zr = zs_ref[pl.ds(c, 1), 0:half]
        zi = zs_ref[pl.ds(c, 1), half:2 * half]
        zs_ref[pl.ds(c, 1), 0:half] = sr
        zs_ref[pl.ds(c, 1), half:2 * half] = si
        return lr * sr - li * si + zr, lr * si + li * sr + zi

    zero = jnp.zeros((1, half), F32)
    lax.fori_loop(0, nch, step, (zero, zero))

    carried = jnp.dot(zs_ref[...].astype(BF16), v_ref[0, 0], preferred_element_type=F32)

    @pl.when(d == 0)
    def _():
        y2_ref[...] = jnp.dot(u2, mt_ref[0], preferred_element_type=F32) + carried

    @pl.when(d == 1)
    def _():
        y2_ref[...] += carried
        for k in range(t):
            y_ref[pl.ds(k, nch, stride=t), :] = y2_ref[:, k * LANES:(k + 1) * LANES]


def _ssm_core(z, ops):
    mt, w, v, decay = ops
    r = z.shape[0]
    t = SSM_T
    nch = r // t
    nb = SSM_WIDTH // LANES
    tl = t * LANES
    ns = 2 * SSM_GB * SSM_STATE
    ucol = 2 * CONV_WIDTH // LANES
    return pl.pallas_call(
        _ssm_kernel,
        grid=(nb, 2),
        in_specs=[pl.BlockSpec((r, LANES), lambda b, d: (0, ucol + b)),
                  pl.BlockSpec((1, tl, tl), lambda b, d: (b, 0, 0)),
                  pl.BlockSpec((1, 1, tl, ns), lambda b, d: (d, b, 0, 0)),
                  pl.BlockSpec((1, 1, ns, tl), lambda b, d: (d, b, 0, 0)),
                  pl.BlockSpec((1, 1, 2, ns // 2), lambda b, d: (d, b, 0, 0))],
        out_specs=pl.BlockSpec((r, LANES), lambda b, d: (0, b)),
        out_shape=jax.ShapeDtypeStruct((r, SSM_WIDTH), F32),
        scratch_shapes=[pltpu.VMEM((nch, tl), BF16), pltpu.VMEM((nch, ns), F32), pltpu.VMEM((nch, tl), F32)],
        compiler_params=_cparams("parallel", "arbitrary"),
        name="ssm_core",
    )(z, mt, w, v, decay)


def _ssm_glu_kernel(u_ref, y_ref, d_ref, w_ref, b_ref, o_ref):
    y = d_ref[...] * u_ref[...] + y_ref[...]
    g = jax.nn.gelu(y)
    gate = jnp.dot(g.astype(BF16), w_ref[...], preferred_element_type=F32) + b_ref[...]
    o_ref[...] = (g * jax.nn.sigmoid(gate)).astype(o_ref.dtype)


def _ssm_glu(z, y, d, w_bf16, b):
    r = z.shape[0]
    t, sw = ROW_TILE, SSM_WIDTH
    ucol = 2 * CONV_WIDTH // sw
    row = pl.BlockSpec((t, sw), lambda i: (i, 0))
    vec = pl.BlockSpec((1, sw), lambda i: (0, 0))
    return pl.pallas_call(
        _ssm_glu_kernel,
        grid=(r // t,),
        in_specs=[pl.BlockSpec((t, sw), lambda i: (i, ucol)), row, vec,
                  pl.BlockSpec((sw, sw), lambda i: (0, 0)), vec],
        out_specs=row,
        out_shape=jax.ShapeDtypeStruct((r, sw), BF16),
        compiler_params=_cparams("parallel"),
        name="ssm_glu",
    )(z, y, d.reshape(1, sw), w_bf16, b.reshape(1, sw))


def _s5_mixer(z, ops, d, glu_w_bf16, glu_b):
    return _ssm_glu(z, _ssm_core(z, ops), d, glu_w_bf16, glu_b)


def _rope_tables():
    rows = SEQ // GRID_W
    row = jnp.repeat(jnp.arange(rows, dtype=F32), GRID_W)
    col = jnp.tile(jnp.arange(GRID_W, dtype=F32), rows)
    inv = jnp.power(ROPE_BASE, -jnp.arange(0, ROPE_AXIS_DIM, 2, dtype=F32) / ROPE_AXIS_DIM)
    ang_r = row[:, None] * inv[None]
    ang_c = col[:, None] * inv[None]
    cos_map = jnp.concatenate([jnp.cos(ang_r), jnp.cos(ang_r), jnp.cos(ang_c), jnp.cos(ang_c)], axis=-1)
    sin_map = jnp.concatenate([-jnp.sin(ang_r), jnp.sin(ang_r), -jnp.sin(ang_c), jnp.sin(ang_c)], axis=-1)
    cos_t = jnp.concatenate([cos_map, cos_map], axis=-1)
    sin_t = jnp.concatenate([sin_map, sin_map], axis=-1)
    cos_t = jnp.concatenate([jnp.ones((CTX_LEN, V_DIM), F32), cos_t], axis=0)
    sin_t = jnp.concatenate([jnp.zeros((CTX_LEN, V_DIM), F32), sin_t], axis=0)
    return cos_t, sin_t


def _rope_kernel(zq_ref, zk_ref, zv_ref, cos_ref, sin_ref, q_ref, k_ref, v_ref):
    cos = cos_ref[...]
    sin = sin_ref[...]
    lane = lax.broadcasted_iota(jnp.int32, cos.shape, 1)
    first_of_pair = (lane & (ROPE_AXIS_DIM // 2)) == 0
    map0 = lane < QK_DIM
    half = ROPE_AXIS_DIM // 2

    def rot(x):
        partner = jnp.where(first_of_pair, pltpu.roll(x, V_DIM - half, axis=1), pltpu.roll(x, half, axis=1))
        return x * cos + partner * sin

    for h in range(zq_ref.shape[1] // V_DIM):
        hs = slice(h * V_DIM, (h + 1) * V_DIM)
        q = rot(zq_ref[:, hs]) * (QK_DIM ** -0.5)
        q_ref[0, :, hs] = jnp.where(map0, q, 0.0).astype(q_ref.dtype)
        q_ref[1, :, hs] = jnp.where(map0, 0.0, q).astype(q_ref.dtype)
        k_ref[:, hs] = rot(zk_ref[:, hs]).astype(k_ref.dtype)
    ones = jnp.ones((zv_ref.shape[0], V_DIM), v_ref.dtype)
    for h in range(zv_ref.shape[1] // V_DIM):
        v_ref[:, 2 * h * V_DIM:(2 * h + 1) * V_DIM] = zv_ref[:, h * V_DIM:(h + 1) * V_DIM].astype(v_ref.dtype)
        v_ref[:, (2 * h + 1) * V_DIM:(2 * h + 2) * V_DIM] = ones


def _rope(z, cos_t, sin_t):
    r = z.shape[0]
    t, w = ROW_TILE, 1024
    qc = (2 * CONV_WIDTH + SSM_WIDTH) // w
    kc = qc + ATTN_WIDTH // w
    vc = kc + ATTN_WIDTH // w
    tab = pl.BlockSpec((t, V_DIM), lambda i, j: (i, 0))
    return pl.pallas_call(
        _rope_kernel,
        grid=(r // t, ATTN_WIDTH // w),
        in_specs=[pl.BlockSpec((t, w), lambda i, j: (i, qc + j)),
                  pl.BlockSpec((t, w), lambda i, j: (i, kc + j)),
                  pl.BlockSpec((t, w), lambda i, j: (i, vc + j)), tab, tab],
        out_specs=[pl.BlockSpec((2, t, w), lambda i, j: (0, i, j)),
                   pl.BlockSpec((t, w), lambda i, j: (i, j)),
                   pl.BlockSpec((t, 2 * w), lambda i, j: (i, j))],
        out_shape=[jax.ShapeDtypeStruct((2, r, ATTN_WIDTH), BF16),
                   jax.ShapeDtypeStruct((r, ATTN_WIDTH), BF16),
                   jax.ShapeDtypeStruct((r, 2 * ATTN_WIDTH), BF16)],
        compiler_params=_cparams("parallel", "parallel"),
        name="rope",
    )(z, z, z, cos_t, sin_t)


def _attn_kernel(lam_ref, q_ref, k_ref, v_ref, g_ref, o_ref, m_ref, acc_ref, s_ref, *, out_scale):
    tq = ATT_TQ
    iq = pl.program_id(1)
    q = q_ref[...].reshape(2 * tq, V_DIM)

    def scores(k):
        return lax.dot_general(q, k, (((1,), (1,)), ((), ())), preferred_element_type=F32)

    def krows(j):
        return k_ref[pl.ds(pl.multiple_of(CTX_LEN + j * ATT_TK, ATT_TK // 2), ATT_TK), :]

    def lane_tiles(s):
        return [s[:, c * LANES:(c + 1) * LANES] for c in range(s.shape[1] // LANES)]

    def consume_first(s, v):
        tiles = lane_tiles(s)
        m_new = jnp.broadcast_to(jnp.max(functools.reduce(jnp.maximum, tiles), axis=-1, keepdims=True),
                                 m_ref.shape)
        p = jnp.concatenate([jnp.exp(t - m_new) for t in tiles], axis=-1).astype(BF16)
        acc_ref[...] = jnp.dot(p, v, preferred_element_type=F32)
        m_ref[...] = m_new

    def consume(s, v):
        tiles = lane_tiles(s)
        m_old = m_ref[...]
        m_new = jnp.maximum(m_old, jnp.max(functools.reduce(jnp.maximum, tiles), axis=-1, keepdims=True))
        alpha = jnp.exp(m_old - m_new)
        p = jnp.concatenate([jnp.exp(t - m_new) for t in tiles], axis=-1).astype(BF16)
        pv = jnp.dot(p, v, preferred_element_type=F32)
        acc_ref[...] = jnp.concatenate([alpha, alpha], axis=-1) * acc_ref[...] + pv
        m_ref[...] = m_new

    def vrows(j):
        return v_ref[pl.ds(pl.multiple_of(CTX_LEN + j * ATT_TK, ATT_TK // 2), ATT_TK), :]

    nlc = SEQ // ATT_TK
    is_ctx = iq < CTX_LEN // tq

    @pl.when(is_ctx)
    def _():
        consume_first(scores(k_ref[0:CTX_LEN, :]), v_ref[0:CTX_LEN, :])

    @pl.when(jnp.logical_not(is_ctx))
    def _():
        s_ref[0] = scores(krows(0))
        consume_first(scores(k_ref[0:CTX_LEN, :]), v_ref[0:CTX_LEN, :])

        def trip(i, carry):
            j = ATT_UNROLL * i
            for u in range(ATT_UNROLL):
                s_ref[(u + 1) % 2] = scores(krows(jnp.minimum(j + u + 1, nlc - 1)))
                consume(s_ref[u % 2], vrows(j + u))
            return carry

        lax.fori_loop(0, nlc // ATT_UNROLL, trip, 0)

    o = acc_ref[:, :V_DIM] / acc_ref[:, V_DIM:]
    od = o[:tq] - lam_ref[0] * o[tq:]
    od = od * lax.rsqrt(jnp.mean(od * od, axis=-1, keepdims=True) + EPS) * g_ref[...]
    o_ref[...] = (od * out_scale).astype(o_ref.dtype)


def _attention(q2, k_r, v, lam, subln_g, lam_init):
    r = v.shape[0]
    tq = ATT_TQ
    assert (SEQ // ATT_TK) % ATT_UNROLL == 0 and ATT_UNROLL % 2 == 0
    return pl.pallas_call(
        functools.partial(_attn_kernel, out_scale=1.0 - lam_init),
        grid=(N_HEADS, r // tq),
        in_specs=[pl.BlockSpec(memory_space=pltpu.SMEM),
                  pl.BlockSpec((2, tq, V_DIM), lambda h, i: (0, i, h)),
                  pl.BlockSpec((r, V_DIM), lambda h, i: (0, h)),
                  pl.BlockSpec((r, 2 * V_DIM), lambda h, i: (0, h)),
                  pl.BlockSpec((1, V_DIM), lambda h, i: (0, 0))],
        out_specs=pl.BlockSpec((tq, V_DIM), lambda h, i: (i, h)),
        out_shape=jax.ShapeDtypeStruct((r, ATTN_WIDTH), BF16),
        scratch_shapes=[pltpu.VMEM((2 * tq, V_DIM), F32), pltpu.VMEM((2 * tq, 2 * V_DIM), F32),
                        pltpu.VMEM((2, 2 * tq, ATT_TK), F32)],
        compiler_params=_cparams("parallel", "arbitrary"),
        name="diff_attention",
    )(lam.reshape(1), q2, k_r, v, subln_g.reshape(1, V_DIM))


def _diff_attention(z, cos_t, sin_t, lq1, lk1, lq2, lk2, subln_g, lam_init):
    lam = (jnp.exp(jnp.sum(lq1.astype(F32) * lk1.astype(F32)))
           - jnp.exp(jnp.sum(lq2.astype(F32) * lk2.astype(F32))) + lam_init)
    q2, k_r, v = _rope(z, cos_t, sin_t)
    return _attention(q2, k_r, v, lam.astype(F32), subln_g.astype(F32), lam_init)


def _modulation_tables(s_pair, mod_down, mod_up, mod_b):
    d = D_MODEL
    low = _matmul(s_pair, mod_down, tm=BF16_SUBLANES, tn=mod_down.shape[1], tk=d, out_dtype=F32, name="mod_down")
    m = _matmul(low, mod_up, tm=BF16_SUBLANES, tn=2048, tk=mod_up.shape[0], out_dtype=F32, name="mod_up")
    m = (m[:2] + mod_b[None]).reshape(2, 6, d)
    return jnp.pad(m, ((0, 0), (0, 2), (0, 0)))


def kernel(x, c, ctx, c_ctx, mod_down, mod_up, mod_b, norm_g, w_in, conv_dw, conv_dw_b, conv_ln_g, conv_ln_b, conv_pw, conv_pw_b, ssm_a_re, ssm_a_im, ssm_log_step, ssm_b_re, ssm_b_im, ssm_c_re, ssm_c_im, ssm_d, ssm_glu_w, ssm_glu_b, lam_q1, lam_k1, lam_q2, lam_k2, attn_subln_g, w_out, mlp_w1, mlp_w2):
    d = D_MODEL
    xs = jnp.concatenate([ctx[0], x[0]], axis=0).astype(F32)
    s_pair = jnp.concatenate([jax.nn.silu(c_ctx)[None], jax.nn.silu(c)], axis=0).astype(F32)
    s_pair = jnp.pad(s_pair, ((0, BF16_SUBLANES - 2), (0, 0)))
    cos_t, sin_t = _rope_tables()

    mods = [_modulation_tables(s_pair, mod_down[l], mod_up[l], mod_b[l]) for l in range(DEPTH)]
    h = _normmod(xs, norm_g[0, 0], mods[0], shift_row=0, scale_row=1)

    w_in_b, w_out_b, w1_b, w2_b = (w.astype(BF16) for w in (w_in, w_out, mlp_w1, mlp_w2))

    for l in range(DEPTH):
        lam_init = 0.8 - 0.6 * math.exp(-0.3 * l)
        mod = mods[l]

        z = _matmul(h, w_in_b, layer=l, tm=MM_TM, tn=512, tk=d, out_dtype=F32, name="w_in")
        y_conv = _conv_mixer(z, conv_dw[l], conv_dw_b[l], conv_ln_g[l], conv_ln_b[l],
                             conv_pw[l].astype(BF16), conv_pw_b[l])
        ops = _ssm_operators(ssm_a_re[l], ssm_a_im[l], ssm_log_step[l], ssm_b_re[l], ssm_b_im[l],
                             ssm_c_re[l], ssm_c_im[l])
        y_ssm = _s5_mixer(z, ops, ssm_d[l], ssm_glu_w[l].astype(BF16), ssm_glu_b[l])
        y_att = _diff_attention(z, cos_t, sin_t, lam_q1[l], lam_k1[l], lam_q2[l], lam_k2[l],
                                attn_subln_g[l], lam_init)
        ycat = jnp.concatenate([y_conv, y_ssm, y_att], axis=-1)
        y = _matmul(ycat, w_out_b, layer=l, tm=MM_TM, tn=512, tk=d, out_dtype=F32, name="w_out")
        xs, h = _resid(y, xs, norm_g[l, 1], norm_g[l, 2], mod, mod,
                       gate_row=2, shift_row=3, scale_row=4, emit_h=True)

        a = _matmul(h, w1_b, layer=l, tm=MM_TM, tn=512, tk=d, out_dtype=BF16, act="sqrelu", name="mlp_w1")
        y = _matmul(a, w2_b, layer=l, tm=MM_TM, tn=1024, tk=2048, out_dtype=F32, name="mlp_w2")
        last = l == DEPTH - 1
        nxt = min(l + 1, DEPTH - 1)
        xs, h = _resid(y, xs, norm_g[l, 3], norm_g[nxt, 0], mod, mods[nxt],
                       gate_row=5, shift_row=0, scale_row=1, emit_h=not last)

    return xs[CTX_LEN:][None].astype(x.dtype)
```

```python
import functools
import math

import jax
import jax.numpy as jnp
from jax import lax
from jax.experimental import pallas as pl
from jax.experimental.pallas import tpu as pltpu

F32 = jnp.float32
BF16 = jnp.bfloat16

D_MODEL = 4096
SEQ = 8192
DEPTH = 4
GRID_W = 64
CTX_LEN = 256
ROWS = CTX_LEN + SEQ
CONV_WIDTH = 1024
SSM_WIDTH = 1024
ATTN_WIDTH = 2048
CONV_KERNEL = 31
CONV_HALF = CONV_KERNEL // 2
SSM_CH = 16
SSM_GROUPS = 64
SSM_STATE = 64
V_DIM = 128
N_HEADS = 16
QK_DIM = 64
ROPE_AXIS_DIM = 32
ROPE_BASE = 10000.0
D_FF = 4 * D_MODEL
EPS = 1e-6
LOG2_E = 1.0 / math.log(2.0)
IN_WIDTH = 2 * CONV_WIDTH + SSM_WIDTH + 3 * ATTN_WIDTH

LANES = 128
BF16_SUBLANES = 16
VMEM_LIMIT = 56 * 1024 * 1024

ROW_TILE = 128
MM_TM = 1056
HALO = 16
CONV_ROWS = 32
SSM_T = 8
SSM_GB = LANES // SSM_CH
ATT_TQ = 256
ATT_TK = 512
ATT_UNROLL = 8


def _cparams(*sem):
    return pltpu.CompilerParams(dimension_semantics=sem, vmem_limit_bytes=VMEM_LIMIT)


def _mm_kernel(a_ref, b_ref, o_ref, *scratch, nk, act):
    def finish(acc):
        if act == "sqrelu":
            acc = jnp.square(jnp.maximum(acc, 0.0))
        o_ref[...] = acc.astype(o_ref.dtype)

    prod = jnp.dot(a_ref[...].astype(BF16), b_ref[...].astype(BF16), preferred_element_type=F32)
    if nk == 1:
        finish(prod)
        return
    acc_ref, = scratch
    k = pl.program_id(2)

    @pl.when(k == 0)
    def _():
        acc_ref[...] = prod

    @pl.when(k > 0)
    def _():
        acc_ref[...] += prod

    @pl.when(k == nk - 1)
    def _():
        finish(acc_ref[...])


def _matmul(a, b, *, tm, tn, tk, out_dtype, act=None, name, layer=None):
    m, kdim = a.shape
    n = b.shape[-1]
    assert m % tm == 0 and n % tn == 0 and kdim % tk == 0 and b.shape[-2] == kdim
    nk = kdim // tk
    scratch = [pltpu.VMEM((tm, tn), F32)] if nk > 1 else []
    if layer is None:
        b_spec = pl.BlockSpec((tk, tn), lambda i, j, k: (k, j))
    else:
        b_spec = pl.BlockSpec((None, tk, tn), lambda i, j, k: (layer, k, j))
    return pl.pallas_call(
        functools.partial(_mm_kernel, nk=nk, act=act),
        grid=(m // tm, n // tn, nk),
        in_specs=[pl.BlockSpec((tm, tk), lambda i, j, k: (i, k)), b_spec],
        out_specs=pl.BlockSpec((tm, tn), lambda i, j, k: (i, j)),
        out_shape=jax.ShapeDtypeStruct((m, n), out_dtype),
        scratch_shapes=scratch,
        compiler_params=_cparams("parallel", "parallel", "arbitrary"),
        name=name,
    )(a, b)


def _mod_index(i):
    return jnp.minimum(i, 1) if CTX_LEN == ROW_TILE else (i >= CTX_LEN // ROW_TILE).astype(jnp.int32)


def _rms(x, g):
    return x * lax.rsqrt(jnp.mean(x * x, axis=-1, keepdims=True) + EPS) * g


def _normmod_kernel(x_ref, g_ref, mod_ref, h_ref, *, shift_row, scale_row):
    mod = mod_ref[0]
    y = _rms(x_ref[...], g_ref[...])
    h = y * (1.0 + mod[scale_row:scale_row + 1]) + mod[shift_row:shift_row + 1]
    h_ref[...] = h.astype(h_ref.dtype)


def _normmod(x, g, mod, *, shift_row, scale_row):
    r, d = x.shape
    return pl.pallas_call(
        functools.partial(_normmod_kernel, shift_row=shift_row, scale_row=scale_row),
        grid=(r // ROW_TILE,),
        in_specs=[pl.BlockSpec((ROW_TILE, d), lambda i: (i, 0)),
                  pl.BlockSpec((1, d), lambda i: (0, 0)),
                  pl.BlockSpec((1, 8, d), lambda i: (_mod_index(i), 0, 0))],
        out_specs=pl.BlockSpec((ROW_TILE, d), lambda i: (i, 0)),
        out_shape=jax.ShapeDtypeStruct((r, d), BF16),
        compiler_params=_cparams("parallel"),
        name="normmod",
    )(x, g.reshape(1, d), mod)


def _resid_kernel(y_ref, x_ref, ga_ref, gb_ref, moda_ref, modb_ref, xo_ref, *h_refs,
                  gate_row, shift_row, scale_row):
    gate = moda_ref[0][gate_row:gate_row + 1]
    xn = x_ref[...] + gate * _rms(y_ref[...], ga_ref[...])
    xo_ref[...] = xn
    if h_refs:
        modb = modb_ref[0]
        h = _rms(xn, gb_ref[...]) * (1.0 + modb[scale_row:scale_row + 1]) + modb[shift_row:shift_row + 1]
        h_refs[0][...] = h.astype(h_refs[0].dtype)


def _resid(y, x, ga, gb, moda, modb, *, gate_row, shift_row, scale_row, emit_h):
    r, d = x.shape
    row = pl.BlockSpec((ROW_TILE, d), lambda i: (i, 0))
    vec = pl.BlockSpec((1, d), lambda i: (0, 0))
    modspec = pl.BlockSpec((1, 8, d), lambda i: (_mod_index(i), 0, 0))
    out_shape = [jax.ShapeDtypeStruct((r, d), F32)]
    out_specs = [row]
    if emit_h:
        out_shape.append(jax.ShapeDtypeStruct((r, d), BF16))
        out_specs.append(row)
    outs = pl.pallas_call(
        functools.partial(_resid_kernel, gate_row=gate_row, shift_row=shift_row, scale_row=scale_row),
        grid=(r // ROW_TILE,),
        in_specs=[row, row, vec, vec, modspec, modspec],
        out_specs=out_specs,
        out_shape=out_shape,
        compiler_params=_cparams("parallel"),
        name="resid",
    )(y, x, ga.reshape(1, d), gb.reshape(1, d), moda, modb)
    return outs if emit_h else (outs[0], None)


def _conv_kernel(ap_ref, ac_ref, an_ref, gp_ref, gc_ref, gn_ref, dw_ref, dwb_ref, lng_ref, lnb_ref,
                 pw_ref, pwb_ref, o_ref, ub_ref, cb_ref, *, nblk, ctx_blocks):
    t = ROW_TILE
    i = pl.program_id(0)
    prev_ok = jnp.logical_and(i != 0, i != ctx_blocks)
    next_ok = jnp.logical_and(i != ctx_blocks - 1, i != nblk - 1)

    def glu(a_ref, g_ref):
        return a_ref[...] * jax.nn.sigmoid(g_ref[...])

    ub_ref[0:HALO, :] = jnp.where(prev_ok, glu(ap_ref, gp_ref), 0.0)
    ub_ref[HALO:HALO + t, :] = glu(ac_ref, gc_ref)
    ub_ref[HALO + t:2 * HALO + t, :] = jnp.where(next_ok, glu(an_ref, gn_ref), 0.0)

    base = HALO - CONV_HALF
    for c in range(CONV_WIDTH // LANES):
        cs = slice(c * LANES, (c + 1) * LANES)
        bias = jnp.broadcast_to(dwb_ref[:, cs], (CONV_ROWS, LANES))
        accs = [bias for _ in range(t // CONV_ROWS)]
        for k in range(CONV_KERNEL):
            wk = jnp.broadcast_to(dw_ref[k:k + 1, cs], (CONV_ROWS, LANES))
            for r in range(t // CONV_ROWS):
                lo = r * CONV_ROWS + base + k
                accs[r] = accs[r] + wk * ub_ref[lo:lo + CONV_ROWS, cs]
        for r in range(t // CONV_ROWS):
            cb_ref[r * CONV_ROWS:(r + 1) * CONV_ROWS, cs] = accs[r]

    cv = cb_ref[...]
    mu = jnp.mean(cv, axis=-1, keepdims=True)
    var = jnp.mean(jnp.square(cv - mu), axis=-1, keepdims=True)
    y = (cv - mu) * lax.rsqrt(var + EPS) * lng_ref[...] + lnb_ref[...]
    y = y * jax.nn.sigmoid(y)
    out = jnp.dot(y.astype(BF16), pw_ref[...], preferred_element_type=F32) + pwb_ref[...]
    o_ref[...] = out.astype(o_ref.dtype)


def _conv_mixer(z, dw, dw_b, ln_g, ln_b, pw_bf16, pw_b):
    r = z.shape[0]
    t, cw = ROW_TILE, CONV_WIDTH
    nblk = r // t
    hb = t // HALO
    nhb = r // HALO

    def cur(col):
        return pl.BlockSpec((t, cw), lambda i: (i, col))

    def prev(col):
        return pl.BlockSpec((HALO, cw), lambda i: (jnp.maximum(i * hb - 1, 0), col))

    def nxt(col):
        return pl.BlockSpec((HALO, cw), lambda i: (jnp.minimum((i + 1) * hb, nhb - 1), col))

    def whole(shape):
        return pl.BlockSpec(shape, lambda i: (0,) * len(shape))

    return pl.pallas_call(
        functools.partial(_conv_kernel, nblk=nblk, ctx_blocks=CTX_LEN // t),
        grid=(nblk,),
        in_specs=[prev(0), cur(0), nxt(0), prev(1), cur(1), nxt(1),
                  whole((CONV_KERNEL, cw)), whole((1, cw)), whole((1, cw)), whole((1, cw)),
                  whole((cw, cw)), whole((1, cw))],
        out_specs=pl.BlockSpec((t, cw), lambda i: (i, 0)),
        out_shape=jax.ShapeDtypeStruct((r, cw), BF16),
        scratch_shapes=[pltpu.VMEM((t + 2 * HALO, cw), F32), pltpu.VMEM((t, cw), F32)],
        compiler_params=_cparams("parallel"),
        name="conv_mixer",
    )(z, z, z, z, z, z, dw, dw_b.reshape(1, cw), ln_g.reshape(1, cw), ln_b.reshape(1, cw),
      pw_bf16, pw_b.reshape(1, cw))


def _ssm_operators(a_re, a_im, log_step, b_re, b_im, c_re, c_im):
    t = SSM_T
    hp = lax.Precision.HIGHEST
    a_re = a_re.astype(F32)
    a_im = a_im.astype(F32)
    dt = jnp.exp(log_step.astype(F32))[..., None]
    mag = jnp.exp(a_re * dt)
    lb_re = mag * jnp.cos(a_im * dt)
    lb_im = mag * jnp.sin(a_im * dt)
    nr, ni = lb_re - 1.0, lb_im
    den = a_re * a_re + a_im * a_im
    f_re = ((nr * a_re + ni * a_im) / den)[..., None]
    f_im = ((ni * a_re - nr * a_im) / den)[..., None]
    b_re = b_re.astype(F32)
    b_im = b_im.astype(F32)
    bb_re = f_re * b_re - f_im * b_im
    bb_im = f_re * b_im + f_im * b_re
    c_re = c_re.astype(F32)
    c_im = c_im.astype(F32)

    k = jnp.arange(t + 1, dtype=F32)[:, None, None, None]
    pmag = jnp.exp(k * (a_re * dt)[None])
    e_re = pmag * jnp.cos(k * (a_im * dt)[None])
    e_im = pmag * jnp.sin(k * (a_im * dt)[None])

    ce_re = c_re[None] * e_re[:, :, :, None, :] - c_im[None] * e_im[:, :, :, None, :]
    ce_im = c_re[None] * e_im[:, :, :, None, :] + c_im[None] * e_re[:, :, :, None, :]
    kk = (jnp.einsum("kdghp,dgpj->kdghj", ce_re, bb_re, precision=hp)
          - jnp.einsum("kdghp,dgpj->kdghj", ce_im, bb_im, precision=hp))

    nb, gl, hh, pp = SSM_GROUPS // SSM_GB, SSM_GB, SSM_CH, SSM_STATE

    lag = jnp.arange(t)[None, :] - jnp.arange(t)[:, None]
    sel = (jnp.where((lag >= 0)[:, :, None, None, None], kk[:, 0][jnp.clip(lag, 0, t)], 0.0)
           + jnp.where((lag <= 0)[:, :, None, None, None], kk[:, 1][jnp.clip(-lag, 0, t)], 0.0))
    mtc = sel.reshape(t, t, nb, gl, hh, hh).transpose(2, 0, 3, 5, 1, 4).reshape(nb, t * LANES, t * hh)

    er = jnp.stack([e_re[t - 1::-1][:t][:, 0], e_re[:t][:, 1]], axis=1)
    ei = jnp.stack([e_im[t - 1::-1][:t][:, 0], e_im[:t][:, 1]], axis=1)
    w_re = er[..., None] * bb_re[None] - ei[..., None] * bb_im[None]
    w_im = er[..., None] * bb_im[None] + ei[..., None] * bb_re[None]
    wc = jnp.stack([w_re, w_im], axis=0).reshape(2, t, 2, nb, gl, pp, hh)
    wc = wc.transpose(2, 3, 1, 4, 6, 0, 5).reshape(2, nb, t * LANES, 2 * pp)

    cr = jnp.stack([ce_re[1:t + 1][:, 0], ce_re[t:0:-1][:, 1]], axis=1)
    ci = jnp.stack([ce_im[1:t + 1][:, 0], ce_im[t:0:-1][:, 1]], axis=1)
    vc = jnp.stack([cr, -ci], axis=0).reshape(2, t, 2, nb, gl, hh, pp)
    vc = vc.transpose(2, 3, 0, 4, 6, 1, 5).reshape(2, nb, 2 * gl * pp, t * hh)

    mt = _expand(mtc, row_unit=hh, col_unit=hh)
    w = _expand(wc, row_unit=hh, col_unit=pp)
    v = _expand(vc, row_unit=pp, col_unit=hh)
    decay = jnp.stack([e_re[t].reshape(2, nb, gl * pp), e_im[t].reshape(2, nb, gl * pp)], axis=2)
    return mt, w, v, decay


def _expand(compact, *, row_unit, col_unit):
    gl = SSM_GB
    rows, cin = compact.shape[-2:]
    cout = cin * gl
    c = jnp.arange(cout)
    src = (c // (gl * col_unit)) * col_unit + c % col_unit
    spread = (jnp.arange(cin)[:, None] == src[None, :]).astype(BF16)
    row_g = (jnp.arange(rows) // row_unit) % gl
    col_g = (c // col_unit) % gl
    mask = (row_g[:, None] == col_g[None, :]).astype(BF16)
    wide = jnp.einsum("...rk,kc->...rc", compact.astype(BF16), spread, preferred_element_type=BF16)
    return wide * mask


def _ssm_kernel(u_ref, mt_ref, w_ref, v_ref, decay_ref, y_ref, u2_ref, zs_ref, y2_ref):
    t = SSM_T
    nch = u2_ref.shape[0]
    nctx = CTX_LEN // t
    half = SSM_GB * SSM_STATE
    d = pl.program_id(1)

    @pl.when(d == 0)
    def _():
        for k in range(t):
            u2_ref[:, k * LANES:(k + 1) * LANES] = u_ref[pl.ds(k, nch, stride=t), :].astype(BF16)

    u2 = u2_ref[...]
    zs_ref[...] = jnp.dot(u2, w_ref[0, 0], preferred_element_type=F32)

    lr = decay_ref[0, 0, 0:1, :]
    li = decay_ref[0, 0, 1:2, :]

    def step(i, carry):
        sr, si = carry
        back = jnp.where(i < nctx, nctx - 1 - i, nch + nctx - 1 - i)
        c = jnp.where(d == 0, i, back)
        zr = zs_ref[pl.ds(c, 1), 0:half]
        zi = zs_ref[pl.ds(c, 1), half:2 * half]
        zs_ref[pl.ds(c, 1), 0:half] = sr
        zs_ref[pl.ds(c, 1), half:2 * half] = si
        return lr * sr - li * si + zr, lr * si + li * sr + zi

    zero = jnp.zeros((1, half), F32)
    lax.fori_loop(0, nch, step, (zero, zero))

    carried = jnp.dot(zs_ref[...].astype(BF16), v_ref[0, 0], preferred_element_type=F32)

    @pl.when(d == 0)
    def _():
        y2_ref[...] = jnp.dot(u2, mt_ref[0], preferred_element_type=F32) + carried

    @pl.when(d == 1)
    def _():
        y2_ref[...] += carried
        for k in range(t):
            y_ref[pl.ds(k, nch, stride=t), :] = y2_ref[:, k * LANES:(k + 1) * LANES]


def _ssm_core(z, ops):
    mt, w, v, decay = ops
    r = z.shape[0]
    t = SSM_T
    nch = r // t
    nb = SSM_WIDTH // LANES
    tl = t * LANES
    ns = 2 * SSM_GB * SSM_STATE
    ucol = 2 * CONV_WIDTH // LANES
    return pl.pallas_call(
        _ssm_kernel,
        grid=(nb, 2),
        in_specs=[pl.BlockSpec((r, LANES), lambda b, d: (0, ucol + b)),
                  pl.BlockSpec((1, tl, tl), lambda b, d: (b, 0, 0)),
                  pl.BlockSpec((1, 1, tl, ns), lambda b, d: (d, b, 0, 0)),
                  pl.BlockSpec((1, 1, ns, tl), lambda b, d: (d, b, 0, 0)),
                  pl.BlockSpec((1, 1, 2, ns // 2), lambda b, d: (d, b, 0, 0))],
        out_specs=pl.BlockSpec((r, LANES), lambda b, d: (0, b)),
        out_shape=jax.ShapeDtypeStruct((r, SSM_WIDTH), F32),
        scratch_shapes=[pltpu.VMEM((nch, tl), BF16), pltpu.VMEM((nch, ns), F32), pltpu.VMEM((nch, tl), F32)],
        compiler_params=_cparams("parallel", "arbitrary"),
        name="ssm_core",
    )(z, mt, w, v, decay)


def _ssm_glu_kernel(u_ref, y_ref, d_ref, w_ref, b_ref, o_ref):
    y = d_ref[...] * u_ref[...] + y_ref[...]
    g = jax.nn.gelu(y)
    gate = jnp.dot(g.astype(BF16), w_ref[...], preferred_element_type=F32) + b_ref[...]
    o_ref[...] = (g * jax.nn.sigmoid(gate)).astype(o_ref.dtype)


def _ssm_glu(z, y, d, w_bf16, b):
    r = z.shape[0]
    t, sw = ROW_TILE, SSM_WIDTH
    ucol = 2 * CONV_WIDTH // sw
    row = pl.BlockSpec((t, sw), lambda i: (i, 0))
    vec = pl.BlockSpec((1, sw), lambda i: (0, 0))
    return pl.pallas_call(
        _ssm_glu_kernel,
        grid=(r // t,),
        in_specs=[pl.BlockSpec((t, sw), lambda i: (i, ucol)), row, vec,
                  pl.BlockSpec((sw, sw), lambda i: (0, 0)), vec],
        out_specs=row,
        out_shape=jax.ShapeDtypeStruct((r, sw), BF16),
        compiler_params=_cparams("parallel"),
        name="ssm_glu",
    )(z, y, d.reshape(1, sw), w_bf16, b.reshape(1, sw))


def _s5_mixer(z, ops, d, glu_w_bf16, glu_b):
    return _ssm_glu(z, _ssm_core(z, ops), d, glu_w_bf16, glu_b)


def _rope_tables():
    rows = SEQ // GRID_W
    row = jnp.repeat(jnp.arange(rows, dtype=F32), GRID_W)
    col = jnp.tile(jnp.arange(GRID_W, dtype=F32), rows)
    inv = jnp.power(ROPE_BASE, -jnp.arange(0, ROPE_AXIS_DIM, 2, dtype=F32) / ROPE_AXIS_DIM)
    ang_r = row[:, None] * inv[None]
    ang_c = col[:, None] * inv[None]
    cos_map = jnp.concatenate([jnp.cos(ang_r), jnp.cos(ang_r), jnp.cos(ang_c), jnp.cos(ang_c)], axis=-1)
    sin_map = jnp.concatenate([-jnp.sin(ang_r), jnp.sin(ang_r), -jnp.sin(ang_c), jnp.sin(ang_c)], axis=-1)
    cos_t = jnp.concatenate([cos_map, cos_map], axis=-1)
    sin_t = jnp.concatenate([sin_map, sin_map], axis=-1)
    cos_t = jnp.concatenate([jnp.ones((CTX_LEN, V_DIM), F32), cos_t], axis=0)
    sin_t = jnp.concatenate([jnp.zeros((CTX_LEN, V_DIM), F32), sin_t], axis=0)
    return cos_t, sin_t


def _rope_kernel(zq_ref, zk_ref, zv_ref, cos_ref, sin_ref, q_ref, k_ref, v_ref):
    cos = cos_ref[...]
    sin = sin_ref[...]
    lane = lax.broadcasted_iota(jnp.int32, cos.shape, 1)
    first_of_pair = (lane & (ROPE_AXIS_DIM // 2)) == 0
    map0 = lane < QK_DIM
    half = ROPE_AXIS_DIM // 2

    def rot(x):
        partner = jnp.where(first_of_pair, pltpu.roll(x, V_DIM - half, axis=1), pltpu.roll(x, half, axis=1))
        return x * cos + partner * sin

    for h in range(zq_ref.shape[1] // V_DIM):
        hs = slice(h * V_DIM, (h + 1) * V_DIM)
        q = rot(zq_ref[:, hs]) * (QK_DIM ** -0.5 * LOG2_E)
        q_ref[0, :, hs] = jnp.where(map0, q, 0.0).astype(q_ref.dtype)
        q_ref[1, :, hs] = jnp.where(map0, 0.0, q).astype(q_ref.dtype)
        k_ref[:, hs] = rot(zk_ref[:, hs]).astype(k_ref.dtype)
    ones = jnp.ones((zv_ref.shape[0], V_DIM), v_ref.dtype)
    for h in range(zv_ref.shape[1] // V_DIM):
        v_ref[:, 2 * h * V_DIM:(2 * h + 1) * V_DIM] = zv_ref[:, h * V_DIM:(h + 1) * V_DIM].astype(v_ref.dtype)
        v_ref[:, (2 * h + 1) * V_DIM:(2 * h + 2) * V_DIM] = ones


def _rope(z, cos_t, sin_t):
    r = z.shape[0]
    t, w = ROW_TILE, 1024
    qc = (2 * CONV_WIDTH + SSM_WIDTH) // w
    kc = qc + ATTN_WIDTH // w
    vc = kc + ATTN_WIDTH // w
    tab = pl.BlockSpec((t, V_DIM), lambda i, j: (i, 0))
    return pl.pallas_call(
        _rope_kernel,
        grid=(r // t, ATTN_WIDTH // w),
        in_specs=[pl.BlockSpec((t, w), lambda i, j: (i, qc + j)),
                  pl.BlockSpec((t, w), lambda i, j: (i, kc + j)),
                  pl.BlockSpec((t, w), lambda i, j: (i, vc + j)), tab, tab],
        out_specs=[pl.BlockSpec((2, t, w), lambda i, j: (0, i, j)),
                   pl.BlockSpec((t, w), lambda i, j: (i, j)),
                   pl.BlockSpec((t, 2 * w), lambda i, j: (i, j))],
        out_shape=[jax.ShapeDtypeStruct((2, r, ATTN_WIDTH), BF16),
                   jax.ShapeDtypeStruct((r, ATTN_WIDTH), BF16),
                   jax.ShapeDtypeStruct((r, 2 * ATTN_WIDTH), BF16)],
        compiler_params=_cparams("parallel", "parallel"),
        name="rope",
    )(z, z, z, cos_t, sin_t)


def _attn_kernel(lam_ref, q_ref, k_ref, v_ref, g_ref, o_ref, m_ref, acc_ref, s_ref, *, out_scale):
    tq = ATT_TQ
    iq = pl.program_id(1)
    q = q_ref[...].reshape(2 * tq, V_DIM)

    def scores(k):
        return lax.dot_general(q, k, (((1,), (1,)), ((), ())), preferred_element_type=F32)

    def krows(j):
        return k_ref[pl.ds(pl.multiple_of(CTX_LEN + j * ATT_TK, ATT_TK // 2), ATT_TK), :]

    def lane_tiles(s):
        return [s[:, c * LANES:(c + 1) * LANES] for c in range(s.shape[1] // LANES)]

    def consume_first(s, v):
        tiles = lane_tiles(s)
        m_new = jnp.broadcast_to(jnp.max(functools.reduce(jnp.maximum, tiles), axis=-1, keepdims=True),
                                 m_ref.shape)
        p = jnp.concatenate([jnp.exp2(t - m_new) for t in tiles], axis=-1).astype(BF16)
        acc_ref[...] = jnp.dot(p, v, preferred_element_type=F32)
        m_ref[...] = m_new

    def consume(s, v):
        tiles = lane_tiles(s)
        m_old = m_ref[...]
        m_new = jnp.maximum(m_old, jnp.max(functools.reduce(jnp.maximum, tiles), axis=-1, keepdims=True))
        alpha = jnp.exp2(m_old - m_new)
        p = jnp.concatenate([jnp.exp2(t - m_new) for t in tiles], axis=-1).astype(BF16)
        pv = jnp.dot(p, v, preferred_element_type=F32)
        acc_ref[...] = jnp.concatenate([alpha, alpha], axis=-1) * acc_ref[...] + pv
        m_ref[...] = m_new

    def vrows(j):
        return v_ref[pl.ds(pl.multiple_of(CTX_LEN + j * ATT_TK, ATT_TK // 2), ATT_TK), :]

    nlc = SEQ // ATT_TK
    is_ctx = iq < CTX_LEN // tq

    @pl.when(is_ctx)
    def _():
        consume_first(scores(k_ref[0:CTX_LEN, :]), v_ref[0:CTX_LEN, :])

    @pl.when(jnp.logical_not(is_ctx))
    def _():
        s_ref[0] = scores(krows(0))
        consume_first(scores(k_ref[0:CTX_LEN, :]), v_ref[0:CTX_LEN, :])

        def trip(i, carry):
            j = ATT_UNROLL * i
            for u in range(ATT_UNROLL):
                s_ref[(u + 1) % 2] = scores(krows(jnp.minimum(j + u + 1, nlc - 1)))
                consume(s_ref[u % 2], vrows(j + u))
            return carry

        lax.fori_loop(0, nlc // ATT_UNROLL, trip, 0)

    o = acc_ref[:, :V_DIM] / acc_ref[:, V_DIM:]
    od = o[:tq] - lam_ref[0] * o[tq:]
    od = od * lax.rsqrt(jnp.mean(od * od, axis=-1, keepdims=True) + EPS) * g_ref[...]
    o_ref[...] = (od * out_scale).astype(o_ref.dtype)


def _attention(q2, k_r, v, lam, subln_g, lam_init):
    r = v.shape[0]
    tq = ATT_TQ
    assert (SEQ // ATT_TK) % ATT_UNROLL == 0 and ATT_UNROLL % 2 == 0
    return pl.pallas_call(
        functools.partial(_attn_kernel, out_scale=1.0 - lam_init),
        grid=(N_HEADS, r // tq),
        in_specs=[pl.BlockSpec(memory_space=pltpu.SMEM),
                  pl.BlockSpec((2, tq, V_DIM), lambda h, i: (0, i, h)),
                  pl.BlockSpec((r, V_DIM), lambda h, i: (0, h)),
                  pl.BlockSpec((r, 2 * V_DIM), lambda h, i: (0, h)),
                  pl.BlockSpec((1, V_DIM), lambda h, i: (0, 0))],
        out_specs=pl.BlockSpec((tq, V_DIM), lambda h, i: (i, h)),
        out_shape=jax.ShapeDtypeStruct((r, ATTN_WIDTH), BF16),
        scratch_shapes=[pltpu.VMEM((2 * tq, V_DIM), F32), pltpu.VMEM((2 * tq, 2 * V_DIM), F32),
                        pltpu.VMEM((2, 2 * tq, ATT_TK), F32)],
        compiler_params=_cparams("parallel", "arbitrary"),
        name="diff_attention",
    )(lam.reshape(1), q2, k_r, v, subln_g.reshape(1, V_DIM))


def _diff_attention(z, cos_t, sin_t, lq1, lk1, lq2, lk2, subln_g, lam_init):
    lam = (jnp.exp(jnp.sum(lq1.astype(F32) * lk1.astype(F32)))
           - jnp.exp(jnp.sum(lq2.astype(F32) * lk2.astype(F32))) + lam_init)
    q2, k_r, v = _rope(z, cos_t, sin_t)
    return _attention(q2, k_r, v, lam.astype(F32), subln_g.astype(F32), lam_init)


def _modulation_tables(s_pair, mod_down, mod_up, mod_b):
    d = D_MODEL
    low = _matmul(s_pair, mod_down, tm=BF16_SUBLANES, tn=mod_down.shape[1], tk=d, out_dtype=F32, name="mod_down")
    m = _matmul(low, mod_up, tm=BF16_SUBLANES, tn=2048, tk=mod_up.shape[0], out_dtype=F32, name="mod_up")
    m = (m[:2] + mod_b[None]).reshape(2, 6, d)
    return jnp.pad(m, ((0, 0), (0, 2), (0, 0)))


def kernel(x, c, ctx, c_ctx, mod_down, mod_up, mod_b, norm_g, w_in, conv_dw, conv_dw_b, conv_ln_g, conv_ln_b, conv_pw, conv_pw_b, ssm_a_re, ssm_a_im, ssm_log_step, ssm_b_re, ssm_b_im, ssm_c_re, ssm_c_im, ssm_d, ssm_glu_w, ssm_glu_b, lam_q1, lam_k1, lam_q2, lam_k2, attn_subln_g, w_out, mlp_w1, mlp_w2):
    d = D_MODEL
    xs = jnp.concatenate([ctx[0], x[0]], axis=0).astype(F32)
    s_pair = jnp.concatenate([jax.nn.silu(c_ctx)[None], jax.nn.silu(c)], axis=0).astype(F32)
    s_pair = jnp.pad(s_pair, ((0, BF16_SUBLANES - 2), (0, 0)))
    cos_t, sin_t = _rope_tables()

    mods = [_modulation_tables(s_pair, mod_down[l], mod_up[l], mod_b[l]) for l in range(DEPTH)]
    h = _normmod(xs, norm_g[0, 0], mods[0], shift_row=0, scale_row=1)

    w_in_b, w_out_b, w1_b, w2_b = (w.astype(BF16) for w in (w_in, w_out, mlp_w1, mlp_w2))

    for l in range(DEPTH):
        lam_init = 0.8 - 0.6 * math.exp(-0.3 * l)
        mod = mods[l]

        z = _matmul(h, w_in_b, layer=l, tm=MM_TM, tn=512, tk=d, out_dtype=F32, name="w_in")
        y_conv = _conv_mixer(z, conv_dw[l], conv_dw_b[l], conv_ln_g[l], conv_ln_b[l],
                             conv_pw[l].astype(BF16), conv_pw_b[l])
        ops = _ssm_operators(ssm_a_re[l], ssm_a_im[l], ssm_log_step[l], ssm_b_re[l], ssm_b_im[l],
                             ssm_c_re[l], ssm_c_im[l])
        y_ssm = _s5_mixer(z, ops, ssm_d[l], ssm_glu_w[l].astype(BF16), ssm_glu_b[l])
        y_att = _diff_attention(z, cos_t, sin_t, lam_q1[l], lam_k1[l], lam_q2[l], lam_k2[l],
                                attn_subln_g[l], lam_init)
        ycat = jnp.concatenate([y_conv, y_ssm, y_att], axis=-1)
        y = _matmul(ycat, w_out_b, layer=l, tm=MM_TM, tn=512, tk=d, out_dtype=F32, name="w_out")
        xs, h = _resid(y, xs, norm_g[l, 1], norm_g[l, 2], mod, mod,
                       gate_row=2, shift_row=3, scale_row=4, emit_h=True)

        a = _matmul(h, w1_b, layer=l, tm=MM_TM, tn=512, tk=d, out_dtype=BF16, act="sqrelu", name="mlp_w1")
        y = _matmul(a, w2_b, layer=l, tm=MM_TM, tn=1024, tk=2048, out_dtype=F32, name="mlp_w2")
        last = l == DEPTH - 1
        nxt = min(l + 1, DEPTH - 1)
        xs, h = _resid(y, xs, norm_g[l, 3], norm_g[nxt, 0], mod, mods[nxt],
                       gate_row=5, shift_row=0, scale_row=1, emit_h=not last)

    return xs[CTX_LEN:][None].astype(x.dtype)
```
